```python
import jax
import jax.numpy as jnp
from jax import lax
import numpy as np

D_MODEL = 2048
BATCH = 4
SEQ = 2048
DEPTH = 2

GRID_W = 64
CTX_LEN = 256
MLA_HEADS = 16
Q_LORA = 512
KV_LORA = 512
QK_NOPE = 128
QK_ROPE = 64
V_DIM = 128
ROPE_BASE = 10000.0
Q_BLOCK = 128
CONV_CH = 1024
CONV_WIDTH = 31
FOURIER_GROUPS = 4
FOURIER_GROUP_CH = 256
FOURIER_CH = FOURIER_GROUPS * FOURIER_GROUP_CH
N_BRANCH = 3
N_EXPERTS = 32
TOP_K = 4
D_EXPERT = 1024
SWIGLU_ALPHA = 1.702
SWIGLU_LIMIT = 7.0
EXPERT_BLOCK = 128
EPS = 1e-6
IN_OFFSETS = (Q_LORA, Q_LORA + KV_LORA, Q_LORA + KV_LORA + QK_ROPE,
              Q_LORA + KV_LORA + QK_ROPE + 2 * CONV_CH,
              Q_LORA + KV_LORA + QK_ROPE + 2 * CONV_CH + FOURIER_CH)
IN_COLS = IN_OFFSETS[-1] + N_BRANCH * D_MODEL

kernel_name = "hybrid_mla_conformer_fnet_moe_dit"


def rms_norm(x, g):
    xf = x.astype(jnp.float32)
    y = xf * lax.rsqrt(jnp.mean(xf * xf, axis=-1, keepdims=True) + EPS)
    return (y * g.astype(jnp.float32)).astype(x.dtype)


def modulate(x, g, shift, scale):
    return rms_norm(x, g) * (1 + scale) + shift


def axial_rope(rows, dtype):
    row = jnp.repeat(jnp.arange(rows, dtype=jnp.float32), GRID_W)
    col = jnp.tile(jnp.arange(GRID_W, dtype=jnp.float32), rows)
    half = QK_ROPE // 2
    inv = ROPE_BASE ** (-jnp.arange(0, half, 2, dtype=jnp.float32) / half)
    ang = jnp.concatenate([row[:, None] * inv, col[:, None] * inv], axis=-1)
    return jnp.cos(ang).astype(dtype), jnp.sin(ang).astype(dtype)


def apply_rope(t, cos, sin):
    t1, t2 = t[..., :QK_ROPE // 2], t[..., QK_ROPE // 2:]
    return jnp.concatenate([t1 * cos - t2 * sin, t2 * cos + t1 * sin], axis=-1)


def mla_q(cq, p):
    b, n, _ = cq.shape
    q = (rms_norm(cq, p['q_norm']) @ p['w_uq']).reshape(b, n, MLA_HEADS, QK_NOPE + QK_ROPE)
    return q[..., :QK_NOPE], q[..., QK_NOPE:]


def mla_kv(ckv, p):
    b, n, _ = ckv.shape
    kv = (rms_norm(ckv, p['kv_norm']) @ p['w_ukv']).reshape(b, n, MLA_HEADS, QK_NOPE + V_DIM)
    return kv[..., :QK_NOPE], kv[..., QK_NOPE:]


def mla_attend(qn, qr, kn, kr, v):
    scale = (QK_NOPE + QK_ROPE) ** -0.5
    s = jnp.einsum('bqhd,bkhd->bhqk', qn, kn) + jnp.einsum('bqhr,bkr->bhqk', qr, kr)
    prob = jax.nn.softmax(s.astype(jnp.float32) * scale, axis=-1).astype(v.dtype)
    return jnp.einsum('bhqk,bkhd->bqhd', prob, v)


def mla_latent(qn, qr, kn, kr, v):
    b, n, h, _ = qn.shape
    nb = n // Q_BLOCK

    def blocks(t):
        return jnp.moveaxis(t.reshape(b, nb, Q_BLOCK, *t.shape[2:]), 1, 0)

    out = lax.map(lambda a: mla_attend(a[0], a[1], kn, kr, v), (blocks(qn), blocks(qr)))
    return jnp.moveaxis(out, 0, 1).reshape(b, n, h * V_DIM)


def conformer_conv(u, p):
    a, g = jnp.split(u, 2, axis=-1)
    z = a * jax.nn.sigmoid(g)
    z = lax.conv_general_dilated(
        z, p['conv_dw'][:, None, :], window_strides=(1,),
        padding=[(CONV_WIDTH // 2, CONV_WIDTH // 2)],
        dimension_numbers=('NWC', 'WIO', 'NWC'), feature_group_count=CONV_CH) + p['conv_dw_b']
    zf = z.astype(jnp.float32)
    mu = jnp.mean(zf, axis=-1, keepdims=True)
    var = jnp.mean(jnp.square(zf - mu), axis=-1, keepdims=True)
    zn = (zf - mu) * lax.rsqrt(var + EPS) * p['conv_ln_g'].astype(jnp.float32) + p['conv_ln_b'].astype(jnp.float32)
    return jax.nn.silu(zn).astype(u.dtype)


def fourier_mix(f):
    b, n, _ = f.shape
    z = f.astype(jnp.float32).reshape(b, n, FOURIER_GROUPS, FOURIER_GROUP_CH)
    z = jnp.fft.fftn(z, axes=(1, 3), norm='ortho').real
    return z.reshape(b, n, FOURIER_CH).astype(f.dtype)


def merge_branches(a, u, f, gt, p):
    y_att = a @ p['w_mla_out']
    y_conv = conformer_conv(u, p) @ p['w_conv_out']
    y_four = fourier_mix(f) @ p['w_four_out']
    gs = jax.nn.sigmoid(gt).reshape(*gt.shape[:-1], N_BRANCH, D_MODEL)
    m = gs[..., 0, :] * y_att + gs[..., 1, :] * y_conv + gs[..., 2, :] * y_four
    return m @ p['w_out']


def moe_ffn(t, p):
    n_tok, d = t.shape
    logits = (t @ p['w_router']).astype(jnp.float32) + p['b_router'].astype(jnp.float32)
    top_v, top_i = lax.top_k(logits, TOP_K)
    gates = jax.nn.softmax(top_v, axis=-1).astype(t.dtype)
    n_pair = n_tok * TOP_K
    e_flat = top_i.reshape(n_pair)
    tok_flat = jnp.repeat(jnp.arange(n_tok, dtype=jnp.int32), TOP_K)
    g_flat = gates.reshape(n_pair)
    order = jnp.argsort(e_flat)
    e_sorted = e_flat[order]
    counts = jnp.bincount(e_flat, length=N_EXPERTS)
    starts = jnp.cumsum(counts) - counts
    padded = (counts + EXPERT_BLOCK - 1) // EXPERT_BLOCK * EXPERT_BLOCK
    pends = jnp.cumsum(padded)
    pstarts = pends - padded
    dest = pstarts[e_sorted] + jnp.arange(n_pair, dtype=jnp.int32) - starts[e_sorted]
    n_blocks = -(-n_pair // EXPERT_BLOCK) + N_EXPERTS
    n_rows = n_blocks * EXPERT_BLOCK
    tok_buf = jnp.zeros((n_rows,), jnp.int32).at[dest].set(tok_flat[order])
    gate_buf = jnp.zeros((n_rows,), t.dtype).at[dest].set(g_flat[order])
    block_e = jnp.minimum(
        jnp.searchsorted(pends, jnp.arange(n_blocks, dtype=jnp.int32) * EXPERT_BLOCK, side='right'),
        N_EXPERTS - 1)
    xb = t[tok_buf].reshape(n_blocks, EXPERT_BLOCK, d)

    def expert_block(args):
        xe, e = args
        hgu = xe @ p['w_gate_up'][e] + p['b_gate_up'][e]
        glu, lin = jnp.split(hgu, 2, axis=-1)
        glu = jnp.minimum(glu, SWIGLU_LIMIT)
        lin = jnp.clip(lin, -SWIGLU_LIMIT, SWIGLU_LIMIT)
        act = glu * jax.nn.sigmoid(SWIGLU_ALPHA * glu) * (lin + 1)
        return act @ p['w_down'][e] + p['b_down'][e]

    yb = lax.map(expert_block, (xb, block_e))
    y = yb.reshape(n_rows, d) * gate_buf[:, None]
    return jnp.zeros_like(t).at[tok_buf].add(y)


def trunk_layer(xl, xc, c_lat, c_ctx, p, cos, sin, last):
    b, n, d = xl.shape
    ctx_len = xc.shape[1]
    mod_l = jax.nn.silu(c_lat) @ p['w_ada'] + p['b_ada']
    mod_c = jax.nn.silu(c_ctx) @ p['w_ada'] + p['b_ada']
    sh1_l, sc1_l, g1_l, sh2_l, sc2_l, g2_l = jnp.split(mod_l[:, None, :], 6, axis=-1)
    sh1_c, sc1_c, g1_c, sh2_c, sc2_c, g2_c = jnp.split(mod_c, 6, axis=-1)

    hl = modulate(xl, p['norm1'], sh1_l, sc1_l)
    hc = modulate(xc, p['norm1'], sh1_c, sc1_c)
    cq_l, ckv_l, kr_l, u_l, f_l, gt_l = jnp.split(hl @ p['w_in'], IN_OFFSETS, axis=-1)
    if last:
        kv_cols = p['w_in'][:, Q_LORA:Q_LORA + KV_LORA + QK_ROPE]
        ckv_c, kr_c = jnp.split(hc @ kv_cols, [KV_LORA], axis=-1)
    else:
        cq_c, ckv_c, kr_c, u_c, f_c, gt_c = jnp.split(hc @ p['w_in'], IN_OFFSETS, axis=-1)
    kn_l, v_l = mla_kv(ckv_l, p)
    kr_l = apply_rope(kr_l, cos, sin)
    qn_l, qr_l = mla_q(cq_l, p)
    qr_l = apply_rope(qr_l, cos[:, None, :], sin[:, None, :])
    kn_c, v_c = mla_kv(ckv_c, p)
    kn = jnp.concatenate([kn_c, kn_l], axis=1)
    kr = jnp.concatenate([kr_c, kr_l], axis=1)
    v = jnp.concatenate([v_c, v_l], axis=1)
    a_l = mla_latent(qn_l, qr_l, kn, kr, v)
    xl = xl + g1_l * merge_branches(a_l, u_l, f_l, gt_l, p)
    if not last:
        qn_c, qr_c = mla_q(cq_c, p)
        a_c = mla_attend(qn_c, qr_c, kn_c, kr_c, v_c).reshape(b, ctx_len, MLA_HEADS * V_DIM)
        xc = xc + g1_c * merge_branches(a_c, u_c, f_c, gt_c, p)

    h2l = modulate(xl, p['norm2'], sh2_l, sc2_l).reshape(b * n, d)
    if last:
        xl = xl + g2_l * moe_ffn(h2l, p).reshape(b, n, d)
    else:
        h2c = modulate(xc, p['norm2'], sh2_c, sc2_c).reshape(b * ctx_len, d)
        y = moe_ffn(jnp.concatenate([h2l, h2c], axis=0), p)
        xl = xl + g2_l * y[:b * n].reshape(b, n, d)
        xc = xc + g2_c * y[b * n:].reshape(b, ctx_len, d)
    return xl, xc


def setup_inputs(seed: int = 0) -> dict:
    key = jax.random.key(seed)
    ks = jax.random.split(key, 32)
    f32 = jnp.float32

    def nrm(k, shape, scale):
        return scale * jax.random.normal(k, shape, f32)

    D, L = D_MODEL, DEPTH
    return {
        'x': nrm(ks[0], (BATCH, SEQ, D), 1.0),
        'c': nrm(ks[1], (BATCH, D), 1.0),
        'ctx': nrm(ks[2], (BATCH, CTX_LEN, D), 1.0),
        'c_ctx': nrm(ks[3], (D,), 1.0),
        'w_ada': nrm(ks[4], (L, D, 6 * D), 0.5 * D ** -0.5),
        'b_ada': nrm(ks[5], (L, 6 * D), 0.02),
        'norm1': 1.0 + nrm(ks[6], (L, D), 0.1),
        'w_in': nrm(ks[7], (L, D, IN_COLS), D ** -0.5),
        'q_norm': 1.0 + nrm(ks[8], (L, Q_LORA), 0.1),
        'kv_norm': 1.0 + nrm(ks[9], (L, KV_LORA), 0.1),
        'w_uq': nrm(ks[10], (L, Q_LORA, MLA_HEADS * (QK_NOPE + QK_ROPE)), Q_LORA ** -0.5),
        'w_ukv': nrm(ks[11], (L, KV_LORA, MLA_HEADS * (QK_NOPE + V_DIM)), KV_LORA ** -0.5),
        'w_mla_out': nrm(ks[12], (L, MLA_HEADS * V_DIM, D), (MLA_HEADS * V_DIM) ** -0.5),
        'conv_dw': nrm(ks[13], (L, CONV_WIDTH, CONV_CH), CONV_WIDTH ** -0.5),
        'conv_dw_b': nrm(ks[14], (L, CONV_CH), 0.02),
        'conv_ln_g': 1.0 + nrm(ks[15], (L, CONV_CH), 0.1),
        'conv_ln_b': nrm(ks[16], (L, CONV_CH), 0.02),
        'w_conv_out': nrm(ks[17], (L, CONV_CH, D), CONV_CH ** -0.5),
        'w_four_out': nrm(ks[18], (L, FOURIER_CH, D), FOURIER_CH ** -0.5),
        'w_out': nrm(ks[19], (L, D, D), D ** -0.5),
        'norm2': 1.0 + nrm(ks[20], (L, D), 0.1),
        'w_router': nrm(ks[21], (L, D, N_EXPERTS), D ** -0.5),
        'b_router': nrm(ks[22], (L, N_EXPERTS), 0.01),
        'w_gate_up': nrm(ks[23], (L, N_EXPERTS, D, 2 * D_EXPERT), D ** -0.5),
        'b_gate_up': nrm(ks[24], (L, N_EXPERTS, 2 * D_EXPERT), 0.02),
        'w_down': nrm(ks[25], (L, N_EXPERTS, D_EXPERT, D), D_EXPERT ** -0.5),
        'b_down': nrm(ks[26], (L, N_EXPERTS, D), 0.02),
        'norm_final': 1.0 + nrm(ks[27], (D,), 0.1),
    }


def reference(x, c, ctx, c_ctx, w_ada, b_ada, norm1, w_in, q_norm, kv_norm, w_uq, w_ukv,
              w_mla_out, conv_dw, conv_dw_b, conv_ln_g, conv_ln_b, w_conv_out, w_four_out,
              w_out, norm2, w_router, b_router, w_gate_up, b_gate_up, w_down, b_down,
              norm_final):
    n = x.shape[1]
    rows = n // GRID_W
    cos, sin = axial_rope(rows, x.dtype)
    xl, xc = x, ctx
    for i in range(DEPTH):
        p = {
            'w_ada': w_ada[i], 'b_ada': b_ada[i], 'norm1': norm1[i], 'w_in': w_in[i],
            'q_norm': q_norm[i], 'kv_norm': kv_norm[i], 'w_uq': w_uq[i], 'w_ukv': w_ukv[i],
            'w_mla_out': w_mla_out[i], 'conv_dw': conv_dw[i], 'conv_dw_b': conv_dw_b[i],
            'conv_ln_g': conv_ln_g[i], 'conv_ln_b': conv_ln_b[i], 'w_conv_out': w_conv_out[i],
            'w_four_out': w_four_out[i], 'w_out': w_out[i], 'norm2': norm2[i],
            'w_router': w_router[i], 'b_router': b_router[i], 'w_gate_up': w_gate_up[i],
            'b_gate_up': b_gate_up[i], 'w_down': w_down[i], 'b_down': b_down[i],
        }
        xl, xc = trunk_layer(xl, xc, c, c_ctx, p, cos, sin, i == DEPTH - 1)
    return rms_norm(xl, norm_final)
```

```python
import functools

import numpy as np
import jax
import jax.numpy as jnp
from jax import lax
from jax.experimental import pallas as pl
from jax.experimental.pallas import tpu as pltpu

F32 = jnp.float32
BF16 = jnp.bfloat16

D = 2048
B = 4
SEQ = 2048
CTX = 256
NB = CTX + SEQ
T = B * NB
TILE = 256
TPB = NB // TILE
NT = T // TILE
GRID_W = 64
H = 16
Q_LORA = 512
KV_LORA = 512
QK_NOPE = 128
QK_ROPE = 64
V_DIM = 128
CONV_CH = 1024
CONV_W = 31
FG = 4
FGC = 256
FCH = FG * FGC
N_EXP = 32
TOP_K = 4
D_EXP = 1024
ALPHA = 1.702
LIMIT = 7.0
EPS = 1e-6
ROPE_BASE = 10000.0
QK_SCALE = float((QK_NOPE + QK_ROPE) ** -0.5)
EXPERT_BM = 256
ROUTER_PAD = 128

ARB = "arbitrary"


def _cparams(n_axes, vmem_mb):
    return pltpu.CompilerParams(dimension_semantics=(ARB,) * n_axes,
                                vmem_limit_bytes=vmem_mb << 20)


def _mod_row(i):
    return jnp.where(i % TPB == 0, B, i // TPB)


def _ada_kernel(c_ref, w_ref, b_ref, o_ref):
    c = c_ref[...]
    s = (c * jax.nn.sigmoid(c)).astype(BF16)
    o_ref[0] = jnp.dot(s, w_ref[0].astype(BF16), preferred_element_type=F32) + b_ref[0]


def ada_mod(cc, w_ada, b_ada):
    L, _, N = w_ada.shape
    tn = 1024
    return pl.pallas_call(
        _ada_kernel,
        grid=(L, N // tn),
        in_specs=[pl.BlockSpec((8, D), lambda l, j: (0, 0)),
                  pl.BlockSpec((1, D, tn), lambda l, j: (l, 0, j)),
                  pl.BlockSpec((1, 1, tn), lambda l, j: (l, 0, j))],
        out_specs=pl.BlockSpec((1, 8, tn), lambda l, j: (l, 0, j)),
        out_shape=jax.ShapeDtypeStruct((L, 8, N), F32),
        compiler_params=_cparams(2, 40),
        name="ada_mod",
    )(cc, w_ada, b_ada.reshape(L, 1, N))


def _rms(x, w):
    ms = jnp.mean(x * x, axis=-1, keepdims=True)
    return x * lax.rsqrt(ms + EPS) * w


def _modnorm_kernel(x_ref, nw_ref, sh_ref, sc_ref, h_ref):
    y = _rms(x_ref[...], nw_ref[...])
    h_ref[...] = (y * (1.0 + sc_ref[0]) + sh_ref[0]).astype(BF16)


def _modnorm_res_kernel(x_ref, y_ref, g_ref, nw_ref, sh_ref, sc_ref, xo_ref, h_ref):
    x = x_ref[...] + g_ref[0] * y_ref[...]
    xo_ref[...] = x
    y = _rms(x, nw_ref[...])
    h_ref[...] = (y * (1.0 + sc_ref[0]) + sh_ref[0]).astype(BF16)


def _vec_spec():
    return pl.BlockSpec((1, 1, D), lambda i: (_mod_row(i), 0, 0))


def _row_spec(cols=D):
    return pl.BlockSpec((TILE, cols), lambda i: (i, 0))


def _const_spec(shape):
    return pl.BlockSpec(shape, lambda i: (0,) * len(shape))


def modnorm(x, nw, sh, sc):
    return pl.pallas_call(
        _modnorm_kernel,
        grid=(NT,),
        in_specs=[_row_spec(), _const_spec((1, D)), _vec_spec(), _vec_spec()],
        out_specs=_row_spec(),
        out_shape=jax.ShapeDtypeStruct((T, D), BF16),
        compiler_params=_cparams(1, 32),
        name="modnorm",
    )(x, nw, sh, sc)


def modnorm_res(x, y, g, nw, sh, sc):
    return pl.pallas_call(
        _modnorm_res_kernel,
        grid=(NT,),
        in_specs=[_row_spec(), _row_spec(), _vec_spec(), _const_spec((1, D)), _vec_spec(),
                  _vec_spec()],
        out_specs=[_row_spec(), _row_spec()],
        out_shape=[jax.ShapeDtypeStruct((T, D), F32), jax.ShapeDtypeStruct((T, D), BF16)],
        compiler_params=_cparams(1, 40),
        name="modnorm_res",
    )(x, y, g, nw, sh, sc)


PROJ_TM = 768


def _proj_plain_kernel(a_ref, w_ref, o_ref):
    o_ref[...] = jnp.dot(a_ref[...], w_ref[...], preferred_element_type=F32).astype(o_ref.dtype)


def _proj_sigmoid_kernel(a_ref, w_ref, o_ref):
    acc = jnp.dot(a_ref[...], w_ref[...], preferred_element_type=F32)
    o_ref[...] = jax.nn.sigmoid(acc).astype(o_ref.dtype)


def _proj_glu_kernel(a_ref, w_ref, o_ref):
    acc = jnp.dot(a_ref[...], w_ref[...], preferred_element_type=F32)
    o_ref[...] = (acc[:, :CONV_CH] * jax.nn.sigmoid(acc[:, CONV_CH:])).astype(o_ref.dtype)


def _proj_four_kernel(a_ref, w_ref, cs_ref, fc_ref, fs_ref):
    f = jnp.dot(a_ref[...], w_ref[...], preferred_element_type=F32).astype(BF16)
    for g in range(FG):
        r = jnp.dot(f[:, g * FGC:(g + 1) * FGC], cs_ref[...], preferred_element_type=F32)
        fc_ref[:, g * FGC:(g + 1) * FGC] = r[:, :FGC].astype(BF16)
        fs_ref[:, g * FGC:(g + 1) * FGC] = r[:, FGC:].astype(BF16)


def proj(kernel, h, w, n_out, tn, name, extra=(), n_outputs=1, vmem_mb=48):
    K, N = w.shape
    tm = PROJ_TM
    tn_out = n_out // (N // tn)
    in_specs = [pl.BlockSpec((tm, K), lambda j, i: (i, 0)),
                pl.BlockSpec((K, tn), lambda j, i: (0, j))]
    for e in extra:
        in_specs.append(pl.BlockSpec(e.shape, lambda j, i, nd=e.ndim: (0,) * nd))
    out_spec = pl.BlockSpec((tm, tn_out), lambda j, i: (i, j))
    out_shape = jax.ShapeDtypeStruct((T, n_out), BF16)
    if n_outputs > 1:
        out_spec = [out_spec] * n_outputs
        out_shape = [out_shape] * n_outputs
    return pl.pallas_call(
        kernel,
        grid=(N // tn, T // tm),
        in_specs=in_specs,
        out_specs=out_spec,
        out_shape=out_shape,
        compiler_params=_cparams(2, vmem_mb),
        name=name,
    )(h, w, *extra)


MLA_TM = 768


def _lane_lt64(shape):
    return lax.broadcasted_iota(jnp.int32, shape, 1) < QK_ROPE


def _qproj_kernel(cq_ref, nw_ref, w_ref, tab_ref, q_ref, cqn_ref):
    @pl.when(pl.program_id(1) == 0)
    def _():
        cqn_ref[...] = _rms(cq_ref[...].astype(F32), nw_ref[...]).astype(BF16)

    y = jnp.dot(cqn_ref[...], w_ref[0], preferred_element_type=F32)
    a = y[:, QK_NOPE:] * tab_ref[...]
    q_ref[0, :, :QK_NOPE] = (y[:, :QK_NOPE] * QK_SCALE).astype(BF16)
    q_ref[0, :, QK_NOPE:] = (a + pltpu.roll(a, QK_ROPE, 1)).astype(BF16)


def q_proj(qkv, nw, wq, tabq):
    tm = MLA_TM
    return pl.pallas_call(
        _qproj_kernel,
        grid=(T // tm, H),
        in_specs=[pl.BlockSpec((tm, Q_LORA), lambda i, h: (i, 0)),
                  pl.BlockSpec((1, Q_LORA), lambda i, h: (0, 0)),
                  pl.BlockSpec((1, Q_LORA, 256), lambda i, h: (h, 0, 0)),
                  pl.BlockSpec((tm, 128), lambda i, h: (i % (NB // tm), 0))],
        out_specs=pl.BlockSpec((1, tm, 256), lambda i, h: (h, i, 0)),
        out_shape=jax.ShapeDtypeStruct((H, T, 256), BF16),
        scratch_shapes=[pltpu.VMEM((tm, Q_LORA), BF16)],
        compiler_params=_cparams(2, 32),
        name="q_proj",
    )(qkv, nw, wq, tabq)


def _kvproj_kernel(ckv_ref, kr_ref, nw_ref, w_ref, tab_ref, k_ref, v_ref, ckvn_ref, k2_ref):
    @pl.when(pl.program_id(1) == 0)
    def _():
        ckvn_ref[...] = _rms(ckv_ref[...].astype(F32), nw_ref[...]).astype(BF16)
        a = kr_ref[...].astype(F32) * tab_ref[...]
        s = a + pltpu.roll(a, QK_ROPE, 1)
        k2_ref[...] = jnp.where(_lane_lt64(s.shape), s, 0.0).astype(BF16)

    y = jnp.dot(ckvn_ref[...], w_ref[0], preferred_element_type=F32)
    k_ref[0, :, :QK_NOPE] = y[:, :QK_NOPE].astype(BF16)
    k_ref[0, :, QK_NOPE:] = k2_ref[...]
    v_ref[0] = y[:, QK_NOPE:].astype(BF16)


def kv_proj(qkv, nw, wkv, tabk):
    tm = MLA_TM
    return pl.pallas_call(
        _kvproj_kernel,
        grid=(T // tm, H),
        in_specs=[pl.BlockSpec((tm, KV_LORA), lambda i, h: (i, 1)),
                  pl.BlockSpec((tm, 128), lambda i, h: (i, (Q_LORA + KV_LORA) // 128)),
                  pl.BlockSpec((1, KV_LORA), lambda i, h: (0, 0)),
                  pl.BlockSpec((1, KV_LORA, 256), lambda i, h: (h, 0, 0)),
                  pl.BlockSpec((tm, 128), lambda i, h: (i % (NB // tm), 0))],
        out_specs=[pl.BlockSpec((1, tm, 256), lambda i, h: (h, i, 0)),
                   pl.BlockSpec((1, tm, V_DIM), lambda i, h: (h, i, 0))],
        out_shape=[jax.ShapeDtypeStruct((H, T, 256), BF16),
                   jax.ShapeDtypeStruct((H, T, V_DIM), BF16)],
        scratch_shapes=[pltpu.VMEM((tm, KV_LORA), BF16), pltpu.VMEM((tm, 128), BF16)],
        compiler_params=_cparams(2, 32),
        name="kv_proj",
    )(qkv, qkv, nw, wkv, tabk)


ATT_TQ = 512


def _attend(q, k, v):
    s = lax.dot_general(q, k, (((1,), (1,)), ((), ())), preferred_element_type=F32)
    m = jnp.max(s, axis=-1, keepdims=True)
    p = jnp.exp(s - m)
    l = jnp.sum(p, axis=-1, keepdims=True)
    o = jnp.dot(p.astype(BF16), v, preferred_element_type=F32)
    return (o / l).astype(BF16)


def _attn_kernel(q_ref, k_ref, v_ref, o_ref):
    o_ref[:CTX, :] = _attend(q_ref[0, :CTX, :], k_ref[0, :CTX, :], v_ref[0, :CTX, :])

    def chunk(c, carry):
        r0 = pl.multiple_of(CTX + c * ATT_TQ, TILE)
        o_ref[pl.ds(r0, ATT_TQ), :] = _attend(q_ref[0, pl.ds(r0, ATT_TQ), :], k_ref[0], v_ref[0])
        return carry

    lax.fori_loop(0, SEQ // ATT_TQ, chunk, 0)


def attention(q, k, v):
    return pl.pallas_call(
        _attn_kernel,
        grid=(B, H),
        in_specs=[pl.BlockSpec((1, NB, 256), lambda b, h: (h, b, 0)),
                  pl.BlockSpec((1, NB, 256), lambda b, h: (h, b, 0)),
                  pl.BlockSpec((1, NB, V_DIM), lambda b, h: (h, b, 0))],
        out_specs=pl.BlockSpec((NB, V_DIM), lambda b, h: (b, h)),
        out_shape=jax.ShapeDtypeStruct((T, H * V_DIM), BF16),
        compiler_params=_cparams(2, 48),
        name="attention",
    )(q, k, v)


HALO = 16
CONV_RC = 32


def _conv_kernel(zp_ref, zc_ref, zn_ref, w_ref, b_ref, g_ref, be_ref, o_ref, buf_ref):
    j = pl.program_id(0) % TPB
    prev_ok = j >= 2
    next_ok = jnp.logical_and(j >= 1, j <= TPB - 2)
    buf_ref[0:HALO, :] = jnp.where(prev_ok, zp_ref[...].astype(F32), 0.0)
    buf_ref[HALO:HALO + TILE, :] = zc_ref[...].astype(F32)
    buf_ref[HALO + TILE:, :] = jnp.where(next_ok, zn_ref[...].astype(F32), 0.0)
    off = HALO - CONV_W // 2
    for rc in range(TILE // CONV_RC):
        r0 = rc * CONV_RC
        acc = jnp.zeros((CONV_RC, CONV_CH), F32) + b_ref[...]
        for t in range(CONV_W):
            acc = acc + buf_ref[r0 + off + t:r0 + off + t + CONV_RC, :] * w_ref[t:t + 1, :]
        mu = jnp.mean(acc, axis=-1, keepdims=True)
        d = acc - mu
        var = jnp.mean(d * d, axis=-1, keepdims=True)
        zn = d * lax.rsqrt(var + EPS) * g_ref[...] + be_ref[...]
        o_ref[r0:r0 + CONV_RC, :] = (zn * jax.nn.sigmoid(zn)).astype(BF16)


def conv_module(z, w_pad, b, g, be):
    nh = T // HALO
    return pl.pallas_call(
        _conv_kernel,
        grid=(NT,),
        in_specs=[pl.BlockSpec((HALO, CONV_CH),
                               lambda i: (jnp.maximum(i * (TILE // HALO) - 1, 0), 0)),
                  pl.BlockSpec((TILE, CONV_CH), lambda i: (i, 0)),
                  pl.BlockSpec((HALO, CONV_CH),
                               lambda i: (jnp.minimum((i + 1) * (TILE // HALO), nh - 1), 0)),
                  _const_spec((32, CONV_CH)), _const_spec((1, CONV_CH)),
                  _const_spec((1, CONV_CH)), _const_spec((1, CONV_CH))],
        out_specs=pl.BlockSpec((TILE, CONV_CH), lambda i: (i, 0)),
        out_shape=jax.ShapeDtypeStruct((T, CONV_CH), BF16),
        scratch_shapes=[pltpu.VMEM((TILE + 2 * HALO, CONV_CH), F32)],
        compiler_params=_cparams(1, 32),
        name="conv_module",
    )(z, z, z, w_pad, b, g, be)


def _dft_tables():
    def cs(n):
        k = np.arange(n, dtype=np.int64)
        ang = 2.0 * np.pi * ((k[:, None] * k[None, :]) % n).astype(np.float64) / n
        return np.cos(ang), np.sin(ang)

    cl, sl = cs(SEQ)
    cc, sc = cs(CTX)
    cg, sg = cs(FGC)
    a_lat = np.concatenate([cl, -sl], axis=1).astype(np.float32)
    a_ctx = np.concatenate([cc, -sc], axis=1).astype(np.float32)
    cs_ch = np.concatenate([cg, sg], axis=1).astype(np.float32)
    return (jnp.asarray(a_lat, dtype=BF16), jnp.asarray(a_ctx, dtype=BF16),
            jnp.asarray(cs_ch, dtype=BF16))


LAT_SCALE = float((SEQ * FGC) ** -0.5)
CTX_SCALE = float((CTX * FGC) ** -0.5)


def _fourier_kernel(al_ref, ac_ref, fc_ref, fs_ref, o_ref):
    m = pl.program_id(1)

    @pl.when(m == 0)
    def _():
        r = jnp.dot(ac_ref[:, :CTX], fc_ref[0, :CTX, :], preferred_element_type=F32)
        r = r + jnp.dot(ac_ref[:, CTX:], fs_ref[0, :CTX, :], preferred_element_type=F32)
        o_ref[...] = (r * CTX_SCALE).astype(BF16)

    @pl.when(m > 0)
    def _():
        r = jnp.dot(al_ref[:, :SEQ], fc_ref[0, CTX:, :], preferred_element_type=F32)
        r = r + jnp.dot(al_ref[:, SEQ:], fs_ref[0, CTX:, :], preferred_element_type=F32)
        o_ref[...] = (r * LAT_SCALE).astype(BF16)


def fourier(fc, fs, a_lat, a_ctx):
    return pl.pallas_call(
        _fourier_kernel,
        grid=(B, TPB),
        in_specs=[pl.BlockSpec((TILE, 2 * SEQ), lambda b, m: (jnp.maximum(m - 1, 0), 0)),
                  pl.BlockSpec((CTX, 2 * CTX), lambda b, m: (0, 0)),
                  pl.BlockSpec((1, NB, FCH), lambda b, m: (b, 0, 0)),
                  pl.BlockSpec((1, NB, FCH), lambda b, m: (b, 0, 0))],
        out_specs=pl.BlockSpec((TILE, FCH), lambda b, m: (b * TPB + m, 0)),
        out_shape=jax.ShapeDtypeStruct((T, FCH), BF16),
        compiler_params=_cparams(2, 48),
        name="fourier",
    )(a_lat, a_ctx, fc.reshape(B, NB, FCH), fs.reshape(B, NB, FCH))


MERGE_TM = 512
MERGE_TN = 1024


def _merge1_kernel(a_ref, cv_ref, fo_ref, g0_ref, g1_ref, g2_ref, wm_ref, wc_ref, wf_ref, o_ref):
    m = g0_ref[...].astype(F32) * jnp.dot(a_ref[...], wm_ref[...], preferred_element_type=F32)
    m = m + g1_ref[...].astype(F32) * jnp.dot(cv_ref[...], wc_ref[...],
                                               preferred_element_type=F32)
    m = m + g2_ref[...].astype(F32) * jnp.dot(fo_ref[...], wf_ref[...],
                                               preferred_element_type=F32)
    o_ref[...] = m.astype(BF16)


def merge1(a, cv, fo, gs, wm, wc, wf):
    tm, tn = MERGE_TM, MERGE_TN
    nj = D // tn

    def gspec(k):
        return pl.BlockSpec((tm, tn), lambda j, i: (i, k * nj + j))

    return pl.pallas_call(
        _merge1_kernel,
        grid=(nj, T // tm),
        in_specs=[pl.BlockSpec((tm, H * V_DIM), lambda j, i: (i, 0)),
                  pl.BlockSpec((tm, CONV_CH), lambda j, i: (i, 0)),
                  pl.BlockSpec((tm, FCH), lambda j, i: (i, 0)),
                  gspec(0), gspec(1), gspec(2),
                  pl.BlockSpec((H * V_DIM, tn), lambda j, i: (0, j)),
                  pl.BlockSpec((CONV_CH, tn), lambda j, i: (0, j)),
                  pl.BlockSpec((FCH, tn), lambda j, i: (0, j))],
        out_specs=pl.BlockSpec((tm, tn), lambda j, i: (i, j)),
        out_shape=jax.ShapeDtypeStruct((T, D), BF16),
        compiler_params=_cparams(2, 48),
        name="merge1",
    )(a, cv, fo, gs, gs, gs, wm, wc, wf)


def _merge2_kernel(m_ref, w_ref, x_ref, g_ref, nw_ref, sh_ref, sc_ref, wr_ref,
                   xo_ref, h_ref, lg_ref):
    acc = jnp.dot(m_ref[...], w_ref[...], preferred_element_type=F32)
    x = x_ref[...] + g_ref[0] * acc
    xo_ref[...] = x
    h = _rms(x, nw_ref[...]) * (1.0 + sc_ref[0]) + sh_ref[0]
    h_ref[...] = h.astype(BF16)
    lg_ref[...] = jnp.dot(h, wr_ref[...], preferred_element_type=F32,
                          precision=lax.Precision.HIGHEST)


def merge2(m, w_out, x, g, nw, sh, sc, wr):
    return pl.pallas_call(
        _merge2_kernel,
        grid=(NT,),
        in_specs=[_row_spec(), _const_spec((D, D)), _row_spec(), _vec_spec(),
                  _const_spec((1, D)), _vec_spec(), _vec_spec(), _const_spec((D, ROUTER_PAD))],
        out_specs=[_row_spec(), _row_spec(), _row_spec(ROUTER_PAD)],
        out_shape=[jax.ShapeDtypeStruct((T, D), F32), jax.ShapeDtypeStruct((T, D), BF16),
                   jax.ShapeDtypeStruct((T, ROUTER_PAD), F32)],
        compiler_params=_cparams(1, 48),
        name="merge2",
    )(m, w_out, x, g, nw, sh, sc, wr)


def _expert_kernel(be_ref, nv_ref, x_ref, wgu_ref, bgu_ref, wd_ref, bd_ref, o_ref):
    i = pl.program_id(0)

    @pl.when(i < nv_ref[0])
    def _():
        hgu = jnp.dot(x_ref[...], wgu_ref[0], preferred_element_type=F32) + bgu_ref[0]
        glu = jnp.minimum(hgu[:, :D_EXP], LIMIT)
        lin = jnp.clip(hgu[:, D_EXP:], -LIMIT, LIMIT)
        act = glu * jax.nn.sigmoid(ALPHA * glu) * (lin + 1.0)
        y = jnp.dot(act.astype(BF16), wd_ref[0], preferred_element_type=F32) + bd_ref[0]
        o_ref[...] = y.astype(BF16)

    @pl.when(i >= nv_ref[0])
    def _():
        o_ref[...] = jnp.zeros(o_ref.shape, BF16)


def experts(block_e, n_valid, xb, wgu, bgu, wd, bd):
    n_rows = xb.shape[0]
    nblk = n_rows // EXPERT_BM

    def xmap(i, be, nv):
        return (jnp.minimum(i, nv[0] - 1), 0)

    def wmap(i, be, nv):
        return (be[i], 0, 0)

    grid_spec = pltpu.PrefetchScalarGridSpec(
        num_scalar_prefetch=2,
        grid=(nblk,),
        in_specs=[pl.BlockSpec((EXPERT_BM, D), xmap),
                  pl.BlockSpec((1, D, 2 * D_EXP), wmap),
                  pl.BlockSpec((1, 1, 2 * D_EXP), wmap),
                  pl.BlockSpec((1, D_EXP, D), wmap),
                  pl.BlockSpec((1, 1, D), wmap)],
        out_specs=pl.BlockSpec((EXPERT_BM, D), lambda i, be, nv: (i, 0)),
    )
    return pl.pallas_call(
        _expert_kernel,
        grid_spec=grid_spec,
        out_shape=jax.ShapeDtypeStruct((n_rows, D), BF16),
        compiler_params=_cparams(1, 52),
        name="experts",
    )(block_e, n_valid, xb, wgu, bgu.reshape(N_EXP, 1, 2 * D_EXP), wd, bd.reshape(N_EXP, 1, D))


def moe(h2, logits, b_router, wgu, bgu, wd, bd, latent_only):
    if latent_only:
        lg = logits.reshape(B, NB, ROUTER_PAD)[:, CTX:, :N_EXP].reshape(B * SEQ, N_EXP)
        tok_ids = (jnp.arange(B, dtype=jnp.int32)[:, None] * NB + CTX
                   + jnp.arange(SEQ, dtype=jnp.int32)[None, :]).reshape(-1)
    else:
        lg = logits[:, :N_EXP]
        tok_ids = jnp.arange(T, dtype=jnp.int32)
    n_tok = lg.shape[0]
    lg = lg + b_router.astype(F32)
    top_v, top_i = lax.top_k(lg, TOP_K)
    gates = jax.nn.softmax(top_v, axis=-1)
    n_pair = n_tok * TOP_K
    e_flat = top_i.reshape(n_pair).astype(jnp.int32)
    oh = (e_flat[:, None] == jnp.arange(N_EXP, dtype=jnp.int32)[None, :]).astype(jnp.int32)
    cs = jnp.cumsum(oh, axis=0)
    rank = jnp.take_along_axis(cs, e_flat[:, None], axis=1)[:, 0] - 1
    counts = cs[-1]
    padded = (counts + EXPERT_BM - 1) // EXPERT_BM * EXPERT_BM
    pends = jnp.cumsum(padded)
    pstarts = pends - padded
    dest = pstarts[e_flat] + rank
    nblk = n_pair // EXPERT_BM + N_EXP
    n_rows = nblk * EXPERT_BM
    tok_flat = jnp.repeat(tok_ids, TOP_K)
    tok_buf = jnp.zeros((n_rows,), jnp.int32).at[dest].set(tok_flat)
    n_valid = (pends[-1] // EXPERT_BM).astype(jnp.int32)
    blk = jnp.arange(nblk, dtype=jnp.int32)
    block_e = jnp.minimum(jnp.searchsorted(pends, blk * EXPERT_BM, side='right'),
                          N_EXP - 1).astype(jnp.int32)
    block_e = jnp.where(blk < n_valid, block_e, block_e[n_valid - 1])
    xb = h2[tok_buf]
    yb = experts(block_e, n_valid.reshape(1), xb, wgu, bgu, wd, bd)
    y = yb[dest].reshape(n_tok, TOP_K, D).astype(F32)
    return jnp.sum(y * gates[:, :, None], axis=1)


def _final_kernel(x_ref, y_ref, g_ref, nw_ref, o_ref):
    x = x_ref[...] + g_ref[0] * y_ref[...]
    o_ref[...] = _rms(x, nw_ref[...])


def final_norm(x, moe_lat, g, nw):
    ntl = SEQ // TILE
    return pl.pallas_call(
        _final_kernel,
        grid=(B, ntl),
        in_specs=[pl.BlockSpec((TILE, D), lambda b, m: (b * TPB + 1 + m, 0)),
                  pl.BlockSpec((TILE, D), lambda b, m: (b * ntl + m, 0)),
                  pl.BlockSpec((1, 1, D), lambda b, m: (b, 0, 0)),
                  pl.BlockSpec((1, D), lambda b, m: (0, 0))],
        out_specs=pl.BlockSpec((TILE, D), lambda b, m: (b * ntl + m, 0)),
        out_shape=jax.ShapeDtypeStruct((B * SEQ, D), F32),
        compiler_params=_cparams(2, 32),
        name="final_norm",
    )(x, moe_lat, g, nw)


def _rope_tables():
    t = np.arange(SEQ)
    row = (t // GRID_W).astype(np.float64)
    col = (t % GRID_W).astype(np.float64)
    half = QK_ROPE // 2
    inv = ROPE_BASE ** (-np.arange(0, half, 2, dtype=np.float64) / half)
    ang = np.concatenate([row[:, None] * inv, col[:, None] * inv], axis=-1)
    cos, sin = np.cos(ang), np.sin(ang)
    lat = np.concatenate([cos, cos, -sin, sin], axis=-1)
    ctx = np.concatenate([np.ones((CTX, QK_ROPE)), np.zeros((CTX, QK_ROPE))], axis=-1)
    tab = np.concatenate([ctx, lat], axis=0).astype(np.float32)
    return jnp.asarray(tab), jnp.asarray(tab * np.float32(QK_SCALE))


def _swap_halves(w):
    half = QK_ROPE // 2
    return jnp.concatenate([w[..., half:], w[..., :half]], axis=-1)


def _layer_weights(w_in, w_uq, w_ukv):
    o0, o1, o2 = Q_LORA, Q_LORA + KV_LORA, Q_LORA + KV_LORA + QK_ROPE
    o3 = o2 + 2 * CONV_CH
    o4 = o3 + FCH
    w_qkv = jnp.concatenate([w_in[:, :o2], _swap_halves(w_in[:, o1:o2])], axis=1).astype(BF16)
    w_u = w_in[:, o2:o3].astype(BF16)
    w_f = w_in[:, o3:o4].astype(BF16)
    w_gt = w_in[:, o4:].astype(BF16)
    wq = w_uq.reshape(Q_LORA, H, QK_NOPE + QK_ROPE)
    wq_r = wq[:, :, QK_NOPE:]
    wq = jnp.concatenate([wq[:, :, :QK_NOPE], wq_r, _swap_halves(wq_r)], axis=-1)
    wq = jnp.transpose(wq, (1, 0, 2)).astype(BF16)
    wkv = jnp.transpose(w_ukv.reshape(KV_LORA, H, QK_NOPE + V_DIM), (1, 0, 2)).astype(BF16)
    return w_qkv, w_u, w_f, w_gt, wq, wkv


def kernel(x, c, ctx, c_ctx, w_ada, b_ada, norm1, w_in, q_norm, kv_norm, w_uq, w_ukv, w_mla_out,
           conv_dw, conv_dw_b, conv_ln_g, conv_ln_b, w_conv_out, w_four_out, w_out, norm2,
           w_router, b_router, w_gate_up, b_gate_up, w_down, b_down, norm_final):
    L = w_ada.shape[0]
    xt = jnp.concatenate([ctx, x], axis=1).reshape(T, D)
    cc = jnp.zeros((8, D), F32).at[:B].set(c).at[B].set(c_ctx)
    mod = ada_mod(cc, w_ada, b_ada).reshape(L, 8, 6, 1, D)
    tabk, tabq = _rope_tables()
    a_lat, a_ctx, cs_ch = _dft_tables()

    moe_out = None
    g2_prev = None
    for l in range(L):
        sh1, sc1, g1, sh2, sc2, g2 = [mod[l, :, k] for k in range(6)]
        w_qkv, w_u, w_f, w_gt, wq, wkv = _layer_weights(w_in[l], w_uq[l], w_ukv[l])
        nw1 = norm1[l].reshape(1, D)
        if l == 0:
            h = modnorm(xt, nw1, sh1, sc1)
        else:
            xt, h = modnorm_res(xt, moe_out, g2_prev, nw1, sh1, sc1)

        qkv = proj(_proj_plain_kernel, h, w_qkv, w_qkv.shape[1], w_qkv.shape[1], "proj_qkv")
        z = proj(_proj_glu_kernel, h, w_u, CONV_CH, 2 * CONV_CH, "proj_glu")
        fc, fs = proj(_proj_four_kernel, h, w_f, FCH, FCH, "proj_four", extra=(cs_ch,),
                      n_outputs=2)
        gs = proj(_proj_sigmoid_kernel, h, w_gt, 3 * D, D, "proj_gate")

        q = q_proj(qkv, q_norm[l].reshape(1, Q_LORA), wq, tabq)
        k, v = kv_proj(qkv, kv_norm[l].reshape(1, KV_LORA), wkv, tabk)
        a = attention(q, k, v)

        w_pad = jnp.zeros((32, CONV_CH), F32).at[:CONV_W].set(conv_dw[l])
        cv = conv_module(z, w_pad, conv_dw_b[l].reshape(1, CONV_CH),
                         conv_ln_g[l].reshape(1, CONV_CH), conv_ln_b[l].reshape(1, CONV_CH))
        fo = fourier(fc, fs, a_lat, a_ctx)

        m = merge1(a, cv, fo, gs, w_mla_out[l].astype(BF16), w_conv_out[l].astype(BF16),
                   w_four_out[l].astype(BF16))
        wr = jnp.zeros((D, ROUTER_PAD), F32).at[:, :N_EXP].set(w_router[l])
        xt, h2, logits = merge2(m, w_out[l].astype(BF16), xt, g1, norm2[l].reshape(1, D), sh2, sc2,
                                wr)
        last = l == L - 1
        moe_out = moe(h2, logits, b_router[l], w_gate_up[l].astype(BF16), b_gate_up[l],
                      w_down[l].astype(BF16), b_down[l], latent_only=last)
        g2_prev = g2

    out = final_norm(xt, moe_out, g2_prev, norm_final.reshape(1, D))
    return out.reshape(B, SEQ, D)
```

```python
import functools

import numpy as np
import jax
import jax.numpy as jnp
from jax import lax
from jax.experimental import pallas as pl
from jax.experimental.pallas import tpu as pltpu

F32 = jnp.float32
BF16 = jnp.bfloat16

D = 2048
B = 4
SEQ = 2048
CTX = 256
NB = CTX + SEQ
T = B * NB
TILE = 256
TPB = NB // TILE
NT = T // TILE
GRID_W = 64
H = 16
Q_LORA = 512
KV_LORA = 512
QK_NOPE = 128
QK_ROPE = 64
V_DIM = 128
CONV_CH = 1024
CONV_W = 31
FG = 4
FGC = 256
FCH = FG * FGC
N_EXP = 32
TOP_K = 4
D_EXP = 1024
ALPHA = 1.702
LIMIT = 7.0
EPS = 1e-6
ROPE_BASE = 10000.0
QK_SCALE = float((QK_NOPE + QK_ROPE) ** -0.5)
EXPERT_BM = 256
ROUTER_PAD = 128
GATHER_SRC_ROWS = 16384

ARB = "arbitrary"


def _cparams(n_axes, vmem_mb):
    return pltpu.CompilerParams(dimension_semantics=(ARB,) * n_axes,
                                vmem_limit_bytes=vmem_mb << 20)


def _mod_row(i):
    return jnp.where(i % TPB == 0, B, i // TPB)


def _ada_kernel(c_ref, w_ref, b_ref, o_ref):
    c = c_ref[...]
    s = (c * jax.nn.sigmoid(c)).astype(BF16)
    o_ref[0] = jnp.dot(s, w_ref[0].astype(BF16), preferred_element_type=F32) + b_ref[0]


def ada_mod(cc, w_ada, b_ada):
    L, _, N = w_ada.shape
    tn = 1024
    return pl.pallas_call(
        _ada_kernel,
        grid=(L, N // tn),
        in_specs=[pl.BlockSpec((8, D), lambda l, j: (0, 0)),
                  pl.BlockSpec((1, D, tn), lambda l, j: (l, 0, j)),
                  pl.BlockSpec((1, 1, tn), lambda l, j: (l, 0, j))],
        out_specs=pl.BlockSpec((1, 8, tn), lambda l, j: (l, 0, j)),
        out_shape=jax.ShapeDtypeStruct((L, 8, N), F32),
        compiler_params=_cparams(2, 40),
        name="ada_mod",
    )(cc, w_ada, b_ada.reshape(L, 1, N))


def _rms(x, w):
    ms = jnp.mean(x * x, axis=-1, keepdims=True)
    return x * lax.rsqrt(ms + EPS) * w


def _modnorm_kernel(x_ref, nw_ref, sh_ref, sc_ref, h_ref):
    y = _rms(x_ref[...], nw_ref[...])
    h_ref[...] = (y * (1.0 + sc_ref[0]) + sh_ref[0]).astype(BF16)


def _vec_spec():
    return pl.BlockSpec((1, 1, D), lambda i: (_mod_row(i), 0, 0))


def _row_spec(cols=D):
    return pl.BlockSpec((TILE, cols), lambda i: (i, 0))


def _const_spec(shape):
    return pl.BlockSpec(shape, lambda i: (0,) * len(shape))


def modnorm(x, nw, sh, sc):
    return pl.pallas_call(
        _modnorm_kernel,
        grid=(NT,),
        in_specs=[_row_spec(), _const_spec((1, D)), _vec_spec(), _vec_spec()],
        out_specs=_row_spec(),
        out_shape=jax.ShapeDtypeStruct((T, D), BF16),
        compiler_params=_cparams(1, 32),
        name="modnorm",
    )(x, nw, sh, sc)


PROJ_TM = 768


def _proj_plain_kernel(a_ref, w_ref, o_ref):
    o_ref[...] = jnp.dot(a_ref[...], w_ref[...], preferred_element_type=F32).astype(o_ref.dtype)


def _proj_sigmoid_kernel(a_ref, w_ref, o_ref):
    acc = jnp.dot(a_ref[...], w_ref[...], preferred_element_type=F32)
    o_ref[...] = jax.nn.sigmoid(acc).astype(o_ref.dtype)


def _proj_glu_kernel(a_ref, w_ref, o_ref):
    acc = jnp.dot(a_ref[...], w_ref[...], preferred_element_type=F32)
    o_ref[...] = (acc[:, :CONV_CH] * jax.nn.sigmoid(acc[:, CONV_CH:])).astype(o_ref.dtype)


def _proj_four_kernel(a_ref, w_ref, cs_ref, fc_ref, fs_ref):
    f = jnp.dot(a_ref[...], w_ref[...], preferred_element_type=F32).astype(BF16)
    cs = cs_ref[...].astype(BF16)
    for g in range(FG):
        r = jnp.dot(f[:, g * FGC:(g + 1) * FGC], cs, preferred_element_type=F32)
        fc_ref[:, g * FGC:(g + 1) * FGC] = r[:, :FGC].astype(BF16)
        fs_ref[:, g * FGC:(g + 1) * FGC] = r[:, FGC:].astype(BF16)


def proj(kernel, h, w, n_out, tn, name, extra=(), n_outputs=1, vmem_mb=48):
    K, N = w.shape
    tm = PROJ_TM
    tn_out = n_out // (N // tn)
    in_specs = [pl.BlockSpec((tm, K), lambda j, i: (i, 0)),
                pl.BlockSpec((K, tn), lambda j, i: (0, j))]
    for e in extra:
        in_specs.append(pl.BlockSpec(e.shape, lambda j, i, nd=e.ndim: (0,) * nd))
    out_spec = pl.BlockSpec((tm, tn_out), lambda j, i: (i, j))
    out_shape = jax.ShapeDtypeStruct((T, n_out), BF16)
    if n_outputs > 1:
        out_spec = [out_spec] * n_outputs
        out_shape = [out_shape] * n_outputs
    return pl.pallas_call(
        kernel,
        grid=(N // tn, T // tm),
        in_specs=in_specs,
        out_specs=out_spec,
        out_shape=out_shape,
        compiler_params=_cparams(2, vmem_mb),
        name=name,
    )(h, w, *extra)


MLA_TM = 768
MLA_HG = 4


def _lane_lt64(shape):
    return lax.broadcasted_iota(jnp.int32, shape, 1) < QK_ROPE


def _qproj_kernel(cq_ref, nw_ref, w_ref, tab_ref, q_ref, cqn_ref):
    @pl.when(pl.program_id(1) == 0)
    def _():
        cqn_ref[...] = _rms(cq_ref[...].astype(F32), nw_ref[...]).astype(BF16)

    for hh in range(MLA_HG):
        y = jnp.dot(cqn_ref[...], w_ref[hh], preferred_element_type=F32)
        a = y[:, QK_NOPE:] * tab_ref[...]
        q_ref[hh, :, :QK_NOPE] = (y[:, :QK_NOPE] * QK_SCALE).astype(BF16)
        q_ref[hh, :, QK_NOPE:] = (a + pltpu.roll(a, QK_ROPE, 1)).astype(BF16)


def q_proj(qkv, nw, wq, tabq):
    tm = MLA_TM
    return pl.pallas_call(
        _qproj_kernel,
        grid=(T // tm, H // MLA_HG),
        in_specs=[pl.BlockSpec((tm, Q_LORA), lambda i, h: (i, 0)),
                  pl.BlockSpec((1, Q_LORA), lambda i, h: (0, 0)),
                  pl.BlockSpec((MLA_HG, Q_LORA, 256), lambda i, h: (h, 0, 0)),
                  pl.BlockSpec((tm, 128), lambda i, h: (i % (NB // tm), 0))],
        out_specs=pl.BlockSpec((MLA_HG, tm, 256), lambda i, h: (h, i, 0)),
        out_shape=jax.ShapeDtypeStruct((H, T, 256), BF16),
        scratch_shapes=[pltpu.VMEM((tm, Q_LORA), BF16)],
        compiler_params=_cparams(2, 32),
        name="q_proj",
    )(qkv, nw, wq, tabq)


def _kvproj_kernel(ckv_ref, kr_ref, nw_ref, w_ref, tab_ref, k_ref, v_ref, ckvn_ref, k2_ref):
    @pl.when(pl.program_id(1) == 0)
    def _():
        ckvn_ref[...] = _rms(ckv_ref[...].astype(F32), nw_ref[...]).astype(BF16)
        a = kr_ref[...].astype(F32) * tab_ref[...]
        s = a + pltpu.roll(a, QK_ROPE, 1)
        k2_ref[...] = jnp.where(_lane_lt64(s.shape), s, 0.0).astype(BF16)

    for hh in range(MLA_HG):
        y = jnp.dot(ckvn_ref[...], w_ref[hh], preferred_element_type=F32)
        k_ref[hh, :, :QK_NOPE] = y[:, :QK_NOPE].astype(BF16)
        k_ref[hh, :, QK_NOPE:] = k2_ref[...]
        v_ref[hh] = y[:, QK_NOPE:].astype(BF16)


def kv_proj(qkv, nw, wkv, tabk):
    tm = MLA_TM
    return pl.pallas_call(
        _kvproj_kernel,
        grid=(T // tm, H // MLA_HG),
        in_specs=[pl.BlockSpec((tm, KV_LORA), lambda i, h: (i, 1)),
                  pl.BlockSpec((tm, 128), lambda i, h: (i, (Q_LORA + KV_LORA) // 128)),
                  pl.BlockSpec((1, KV_LORA), lambda i, h: (0, 0)),
                  pl.BlockSpec((MLA_HG, KV_LORA, 256), lambda i, h: (h, 0, 0)),
                  pl.BlockSpec((tm, 128), lambda i, h: (i % (NB // tm), 0))],
        out_specs=[pl.BlockSpec((MLA_HG, tm, 256), lambda i, h: (h, i, 0)),
                   pl.BlockSpec((MLA_HG, tm, V_DIM), lambda i, h: (h, i, 0))],
        out_shape=[jax.ShapeDtypeStruct((H, T, 256), BF16),
                   jax.ShapeDtypeStruct((H, T, V_DIM), BF16)],
        scratch_shapes=[pltpu.VMEM((tm, KV_LORA), BF16), pltpu.VMEM((tm, 128), BF16)],
        compiler_params=_cparams(2, 32),
        name="kv_proj",
    )(qkv, qkv, nw, wkv, tabk)


ATT_TQ = 256


def _attend(q, k, v):
    s = lax.dot_general(q, k, (((1,), (1,)), ((), ())), preferred_element_type=F32)
    m = jnp.max(s, axis=-1, keepdims=True)
    p = jnp.exp(s - m)
    l = jnp.sum(p, axis=-1, keepdims=True)
    o = jnp.dot(p.astype(BF16), v, preferred_element_type=F32)
    return (o / l).astype(BF16)


def _attn_kernel(q_ref, k_ref, v_ref, o_ref):
    o_ref[:CTX, :] = _attend(q_ref[0, :CTX, :], k_ref[0, :CTX, :], v_ref[0, :CTX, :])
    for c in range(SEQ // ATT_TQ):
        r0 = CTX + c * ATT_TQ
        o_ref[r0:r0 + ATT_TQ, :] = _attend(q_ref[0, r0:r0 + ATT_TQ, :], k_ref[0], v_ref[0])


def attention(q, k, v):
    return pl.pallas_call(
        _attn_kernel,
        grid=(B, H),
        in_specs=[pl.BlockSpec((1, NB, 256), lambda b, h: (h, b, 0)),
                  pl.BlockSpec((1, NB, 256), lambda b, h: (h, b, 0)),
                  pl.BlockSpec((1, NB, V_DIM), lambda b, h: (h, b, 0))],
        out_specs=pl.BlockSpec((NB, V_DIM), lambda b, h: (b, h)),
        out_shape=jax.ShapeDtypeStruct((T, H * V_DIM), BF16),
        compiler_params=_cparams(2, 48),
        name="attention",
    )(q, k, v)


HALO = 16
CONV_RC = 32
SUBLANES = 8
CONV_SH_ROWS = TILE + 2 * HALO - SUBLANES


def _conv_kernel(zp_ref, zc_ref, zn_ref, w_ref, b_ref, g_ref, be_ref, o_ref, buf_ref, sh_ref):
    j = pl.program_id(0) % TPB
    prev_ok = j >= 2
    next_ok = jnp.logical_and(j >= 1, j <= TPB - 2)
    buf_ref[0:HALO, :] = jnp.where(prev_ok, zp_ref[...].astype(F32), 0.0)
    buf_ref[HALO:HALO + TILE, :] = zc_ref[...].astype(F32)
    buf_ref[HALO + TILE:, :] = jnp.where(next_ok, zn_ref[...].astype(F32), 0.0)
    for s in range(1, SUBLANES):
        sh_ref[s - 1] = buf_ref[s:s + CONV_SH_ROWS, :]
    off = HALO - CONV_W // 2
    for rc in range(TILE // CONV_RC):
        r0 = rc * CONV_RC
        acc = jnp.zeros((CONV_RC, CONV_CH), F32) + b_ref[...]
        for t in range(CONV_W):
            q, s = divmod(off + t, SUBLANES)
            a0 = r0 + q * SUBLANES
            if s == 0:
                tap = buf_ref[a0:a0 + CONV_RC, :]
            else:
                tap = sh_ref[s - 1, a0:a0 + CONV_RC, :]
            acc = acc + tap * w_ref[t:t + 1, :]
        mu = jnp.mean(acc, axis=-1, keepdims=True)
        d = acc - mu
        var = jnp.mean(d * d, axis=-1, keepdims=True)
        zn = d * lax.rsqrt(var + EPS) * g_ref[...] + be_ref[...]
        o_ref[r0:r0 + CONV_RC, :] = (zn * jax.nn.sigmoid(zn)).astype(BF16)


def conv_module(z, w_pad, b, g, be):
    nh = T // HALO
    return pl.pallas_call(
        _conv_kernel,
        grid=(NT,),
        in_specs=[pl.BlockSpec((HALO, CONV_CH),
                               lambda i: (jnp.maximum(i * (TILE // HALO) - 1, 0), 0)),
                  pl.BlockSpec((TILE, CONV_CH), lambda i: (i, 0)),
                  pl.BlockSpec((HALO, CONV_CH),
                               lambda i: (jnp.minimum((i + 1) * (TILE // HALO), nh - 1), 0)),
                  _const_spec((32, CONV_CH)), _const_spec((1, CONV_CH)),
                  _const_spec((1, CONV_CH)), _const_spec((1, CONV_CH))],
        out_specs=pl.BlockSpec((TILE, CONV_CH), lambda i: (i, 0)),
        out_shape=jax.ShapeDtypeStruct((T, CONV_CH), BF16),
        scratch_shapes=[pltpu.VMEM((TILE + 2 * HALO, CONV_CH), F32),
                        pltpu.VMEM((SUBLANES - 1, CONV_SH_ROWS, CONV_CH), F32)],
        compiler_params=_cparams(1, 40),
        name="conv_module",
    )(z, z, z, w_pad, b, g, be)


def _dft_tables():
    def cs(n):
        k = np.arange(n, dtype=np.int64)
        ang = 2.0 * np.pi * ((k[:, None] * k[None, :]) % n).astype(np.float64) / n
        return np.cos(ang), np.sin(ang)

    cl, sl = cs(SEQ)
    cc, sc = cs(CTX)
    cg, sg = cs(FGC)
    a_lat = np.concatenate([cl, -sl], axis=1).astype(np.float32)
    a_ctx = np.concatenate([cc, -sc], axis=1).astype(np.float32)
    cs_ch = np.concatenate([cg, sg], axis=1).astype(np.float32)
    return jnp.asarray(a_lat), jnp.asarray(a_ctx), jnp.asarray(cs_ch)


LAT_SCALE = float((SEQ * FGC) ** -0.5)
CTX_SCALE = float((CTX * FGC) ** -0.5)


def _fourier_kernel(al_ref, ac_ref, fc_ref, fs_ref, o_ref):
    m = pl.program_id(1)

    @pl.when(m == 0)
    def _():
        r = jnp.dot(ac_ref[:, :CTX].astype(BF16), fc_ref[0, :CTX, :], preferred_element_type=F32)
        r = r + jnp.dot(ac_ref[:, CTX:].astype(BF16), fs_ref[0, :CTX, :],
                        preferred_element_type=F32)
        o_ref[...] = (r * CTX_SCALE).astype(BF16)

    @pl.when(m > 0)
    def _():
        r = jnp.dot(al_ref[:, :SEQ].astype(BF16), fc_ref[0, CTX:, :], preferred_element_type=F32)
        r = r + jnp.dot(al_ref[:, SEQ:].astype(BF16), fs_ref[0, CTX:, :],
                        preferred_element_type=F32)
        o_ref[...] = (r * LAT_SCALE).astype(BF16)


def fourier(fc, fs, a_lat, a_ctx):
    return pl.pallas_call(
        _fourier_kernel,
        grid=(B, TPB),
        in_specs=[pl.BlockSpec((TILE, 2 * SEQ), lambda b, m: (jnp.maximum(m - 1, 0), 0)),
                  pl.BlockSpec((CTX, 2 * CTX), lambda b, m: (0, 0)),
                  pl.BlockSpec((1, NB, FCH), lambda b, m: (b, 0, 0)),
                  pl.BlockSpec((1, NB, FCH), lambda b, m: (b, 0, 0))],
        out_specs=pl.BlockSpec((TILE, FCH), lambda b, m: (b * TPB + m, 0)),
        out_shape=jax.ShapeDtypeStruct((T, FCH), BF16),
        compiler_params=_cparams(2, 48),
        name="fourier",
    )(a_lat, a_ctx, fc.reshape(B, NB, FCH), fs.reshape(B, NB, FCH))


MERGE_TM = 512
MERGE_TN = 1024


def _merge1_kernel(a_ref, cv_ref, fo_ref, g0_ref, g1_ref, g2_ref, wm_ref, wc_ref, wf_ref, o_ref):
    m = g0_ref[...].astype(F32) * jnp.dot(a_ref[...], wm_ref[...], preferred_element_type=F32)
    m = m + g1_ref[...].astype(F32) * jnp.dot(cv_ref[...], wc_ref[...],
                                               preferred_element_type=F32)
    m = m + g2_ref[...].astype(F32) * jnp.dot(fo_ref[...], wf_ref[...],
                                               preferred_element_type=F32)
    o_ref[...] = m.astype(BF16)


def merge1(a, cv, fo, gs, wm, wc, wf):
    tm, tn = MERGE_TM, MERGE_TN
    nj = D // tn

    def gspec(k):
        return pl.BlockSpec((tm, tn), lambda j, i: (i, k * nj + j))

    return pl.pallas_call(
        _merge1_kernel,
        grid=(nj, T // tm),
        in_specs=[pl.BlockSpec((tm, H * V_DIM), lambda j, i: (i, 0)),
                  pl.BlockSpec((tm, CONV_CH), lambda j, i: (i, 0)),
                  pl.BlockSpec((tm, FCH), lambda j, i: (i, 0)),
                  gspec(0), gspec(1), gspec(2),
                  pl.BlockSpec((H * V_DIM, tn), lambda j, i: (0, j)),
                  pl.BlockSpec((CONV_CH, tn), lambda j, i: (0, j)),
                  pl.BlockSpec((FCH, tn), lambda j, i: (0, j))],
        out_specs=pl.BlockSpec((tm, tn), lambda j, i: (i, j)),
        out_shape=jax.ShapeDtypeStruct((T, D), BF16),
        compiler_params=_cparams(2, 48),
        name="merge1",
    )(a, cv, fo, gs, gs, gs, wm, wc, wf)


def _merge2_kernel(m_ref, w_ref, x_ref, g_ref, nw_ref, sh_ref, sc_ref, wr_ref,
                   xo_ref, h_ref, lg_ref):
    acc = jnp.dot(m_ref[...], w_ref[...], preferred_element_type=F32)
    x = x_ref[...] + g_ref[0] * acc
    xo_ref[...] = x
    h = (_rms(x, nw_ref[...]) * (1.0 + sc_ref[0]) + sh_ref[0]).astype(BF16)
    h_ref[...] = h
    lg_ref[...] = jnp.dot(h, wr_ref[...], preferred_element_type=F32)


def merge2(m, w_out, x, g, nw, sh, sc, wr):
    return pl.pallas_call(
        _merge2_kernel,
        grid=(NT,),
        in_specs=[_row_spec(), _const_spec((D, D)), _row_spec(), _vec_spec(),
                  _const_spec((1, D)), _vec_spec(), _vec_spec(), _const_spec((D, ROUTER_PAD))],
        out_specs=[_row_spec(), _row_spec(), _row_spec(ROUTER_PAD)],
        out_shape=[jax.ShapeDtypeStruct((T, D), F32), jax.ShapeDtypeStruct((T, D), BF16),
                   jax.ShapeDtypeStruct((T, ROUTER_PAD), F32)],
        compiler_params=_cparams(1, 48),
        name="merge2",
    )(m, w_out, x, g, nw, sh, sc, wr)


def _expert_kernel(be_ref, nv_ref, x_ref, wgu_ref, bgu_ref, wd_ref, bd_ref, o_ref):
    i = pl.program_id(0)

    @pl.when(i < nv_ref[0])
    def _():
        hgu = jnp.dot(x_ref[...], wgu_ref[0], preferred_element_type=F32) + bgu_ref[0]
        glu = jnp.minimum(hgu[:, :D_EXP], LIMIT)
        lin = jnp.clip(hgu[:, D_EXP:], -LIMIT, LIMIT)
        act = glu * jax.nn.sigmoid(ALPHA * glu) * (lin + 1.0)
        y = jnp.dot(act.astype(BF16), wd_ref[0], preferred_element_type=F32) + bd_ref[0]
        o_ref[...] = y.astype(BF16)

    @pl.when(i >= nv_ref[0])
    def _():
        o_ref[...] = jnp.zeros(o_ref.shape, BF16)


def experts(block_e, n_valid, xb, wgu, bgu, wd, bd):
    n_rows = xb.shape[0]
    nblk = n_rows // EXPERT_BM

    def xmap(i, be, nv):
        return (jnp.minimum(i, nv[0] - 1), 0)

    def wmap(i, be, nv):
        return (be[i], 0, 0)

    grid_spec = pltpu.PrefetchScalarGridSpec(
        num_scalar_prefetch=2,
        grid=(nblk,),
        in_specs=[pl.BlockSpec((EXPERT_BM, D), xmap),
                  pl.BlockSpec((1, D, 2 * D_EXP), wmap),
                  pl.BlockSpec((1, 1, 2 * D_EXP), wmap),
                  pl.BlockSpec((1, D_EXP, D), wmap),
                  pl.BlockSpec((1, 1, D), wmap)],
        out_specs=pl.BlockSpec((EXPERT_BM, D), lambda i, be, nv: (i, 0)),
    )
    return pl.pallas_call(
        _expert_kernel,
        grid_spec=grid_spec,
        out_shape=jax.ShapeDtypeStruct((n_rows, D), BF16),
        compiler_params=_cparams(1, 52),
        name="experts",
    )(block_e, n_valid, xb, wgu, bgu.reshape(N_EXP, 1, 2 * D_EXP), wd, bd.reshape(N_EXP, 1, D))


NEG_BIG = -1e30
RO_E, RO_RANK, RO_GATE = 0, TOP_K, 2 * TOP_K


def _router_kernel(lg_ref, b_ref, tri_ref, ro_ref, cnt_ref, base_ref):
    @pl.when(pl.program_id(0) == 0)
    def _():
        base_ref[...] = jnp.zeros(base_ref.shape, F32)

    lane = lax.broadcasted_iota(jnp.int32, (TILE, ROUTER_PAD), 1)
    lanef = lane.astype(F32)
    lg = jnp.where(lane < N_EXP, lg_ref[...] + b_ref[...], NEG_BIG)
    vals, idxs, hots = [], [], []
    for _ in range(TOP_K):
        m = jnp.max(lg, axis=-1, keepdims=True)
        idx = jnp.min(jnp.where(lg == m, lanef, float(ROUTER_PAD)), axis=-1, keepdims=True)
        hot = lanef == idx
        lg = jnp.where(hot, NEG_BIG, lg)
        vals.append(m)
        idxs.append(idx)
        hots.append(hot)
    ex = [jnp.exp(v - vals[0]) for v in vals]
    den = ex[0] + ex[1] + ex[2] + ex[3]
    chosen = jnp.zeros((TILE, ROUTER_PAD), F32)
    for hot in hots:
        chosen = jnp.where(hot, 1.0, chosen)
    before = jnp.dot(tri_ref[...], chosen.astype(BF16), preferred_element_type=F32)
    tot = before + base_ref[0:1, :]
    rec = jnp.zeros((TILE, ROUTER_PAD), F32)
    for k in range(TOP_K):
        rank = jnp.sum(jnp.where(hots[k], tot, 0.0), axis=-1, keepdims=True)
        rec = jnp.where(lane == RO_E + k, idxs[k], rec)
        rec = jnp.where(lane == RO_RANK + k, rank, rec)
        rec = jnp.where(lane == RO_GATE + k, ex[k] / den, rec)
    ro_ref[...] = rec
    base_ref[...] = base_ref[...] + jnp.sum(chosen, axis=0, keepdims=True)
    cnt_ref[...] = base_ref[...]


def router(logits, b_pad, tri, latent_only):
    if latent_only:
        ntl = SEQ // TILE
        n_tiles = B * ntl
        in_map = lambda i: ((i // ntl) * TPB + 1 + i % ntl, 0)
    else:
        n_tiles = NT
        in_map = lambda i: (i, 0)
    return pl.pallas_call(
        _router_kernel,
        grid=(n_tiles,),
        in_specs=[pl.BlockSpec((TILE, ROUTER_PAD), in_map),
                  _const_spec((1, ROUTER_PAD)), _const_spec((TILE, TILE))],
        out_specs=[_row_spec(ROUTER_PAD), _const_spec((8, ROUTER_PAD))],
        out_shape=[jax.ShapeDtypeStruct((n_tiles * TILE, ROUTER_PAD), F32),
                   jax.ShapeDtypeStruct((8, ROUTER_PAD), F32)],
        scratch_shapes=[pltpu.VMEM((8, ROUTER_PAD), F32)],
        compiler_params=_cparams(1, 32),
        name="router",
    )(logits, b_pad, tri)


def moe(h2, logits, b_router, wgu, bgu, wd, bd, latent_only):
    b_pad = jnp.zeros((1, ROUTER_PAD), F32).at[0, :N_EXP].set(b_router)
    tri = jnp.asarray(np.tril(np.ones((TILE, TILE), np.float32), -1), dtype=BF16)
    ro, cnt = router(logits, b_pad, tri, latent_only)
    n_tok = ro.shape[0]
    if latent_only:
        tok_ids = (jnp.arange(B, dtype=jnp.int32)[:, None] * NB + CTX
                   + jnp.arange(SEQ, dtype=jnp.int32)[None, :]).reshape(-1)
    else:
        tok_ids = jnp.arange(T, dtype=jnp.int32)
    e = ro[:, RO_E:RO_E + TOP_K].astype(jnp.int32)
    rank = ro[:, RO_RANK:RO_RANK + TOP_K].astype(jnp.int32)
    counts = cnt[0, :N_EXP].astype(jnp.int32)
    padded = (counts + EXPERT_BM - 1) // EXPERT_BM * EXPERT_BM
    pends = jnp.cumsum(padded)
    pstarts = pends - padded
    dest = pstarts[e] + rank
    n_pair = n_tok * TOP_K
    nblk = n_pair // EXPERT_BM + N_EXP
    n_rows = nblk * EXPERT_BM
    tok_buf = jnp.zeros((n_rows,), jnp.int32).at[dest.reshape(-1)].set(jnp.repeat(tok_ids, TOP_K))
    n_valid = (pends[-1] // EXPERT_BM).astype(jnp.int32)
    blk = jnp.arange(nblk, dtype=jnp.int32)
    block_e = jnp.minimum(jnp.sum((pends[None, :] <= blk[:, None] * EXPERT_BM).astype(jnp.int32),
                                  axis=1), N_EXP - 1)
    block_e = jnp.where(blk < n_valid, block_e, block_e[n_valid - 1])
    h2_src = jnp.concatenate([h2, jnp.zeros((GATHER_SRC_ROWS - T, D), BF16)], axis=0)
    xb = h2_src[tok_buf]
    yb = experts(block_e, n_valid.reshape(1), xb, wgu, bgu, wd, bd)
    ys = [yb[dest[:, k]] for k in range(TOP_K)]
    return ro, ys


def _combine(ro_ref, y_refs):
    ro = ro_ref[...]
    acc = ro[:, RO_GATE:RO_GATE + 1] * y_refs[0][...].astype(F32)
    for k in range(1, TOP_K):
        acc = acc + ro[:, RO_GATE + k:RO_GATE + k + 1] * y_refs[k][...].astype(F32)
    return acc


def _modnorm_moe_kernel(x_ref, ro_ref, y0_ref, y1_ref, y2_ref, y3_ref, g_ref, nw_ref, sh_ref,
                        sc_ref, xo_ref, h_ref):
    x = x_ref[...] + g_ref[0] * _combine(ro_ref, (y0_ref, y1_ref, y2_ref, y3_ref))
    xo_ref[...] = x
    y = _rms(x, nw_ref[...])
    h_ref[...] = (y * (1.0 + sc_ref[0]) + sh_ref[0]).astype(BF16)


def modnorm_moe(x, ro, ys, g, nw, sh, sc):
    return pl.pallas_call(
        _modnorm_moe_kernel,
        grid=(NT,),
        in_specs=[_row_spec(), _row_spec(ROUTER_PAD)] + [_row_spec()] * TOP_K
                 + [_vec_spec(), _const_spec((1, D)), _vec_spec(), _vec_spec()],
        out_specs=[_row_spec(), _row_spec()],
        out_shape=[jax.ShapeDtypeStruct((T, D), F32), jax.ShapeDtypeStruct((T, D), BF16)],
        compiler_params=_cparams(1, 40),
        name="modnorm_moe",
    )(x, ro, *ys, g, nw, sh, sc)


def _final_kernel(x_ref, ro_ref, y0_ref, y1_ref, y2_ref, y3_ref, g_ref, nw_ref, o_ref):
    x = x_ref[...] + g_ref[0] * _combine(ro_ref, (y0_ref, y1_ref, y2_ref, y3_ref))
    o_ref[...] = _rms(x, nw_ref[...])


def final_norm(x, ro, ys, g, nw):
    ntl = SEQ // TILE
    lat = lambda cols: pl.BlockSpec((TILE, cols), lambda b, m: (b * ntl + m, 0))
    return pl.pallas_call(
        _final_kernel,
        grid=(B, ntl),
        in_specs=[pl.BlockSpec((TILE, D), lambda b, m: (b * TPB + 1 + m, 0)),
                  lat(ROUTER_PAD)] + [lat(D)] * TOP_K
                 + [pl.BlockSpec((1, 1, D), lambda b, m: (b, 0, 0)),
                    pl.BlockSpec((1, D), lambda b, m: (0, 0))],
        out_specs=lat(D),
        out_shape=jax.ShapeDtypeStruct((B * SEQ, D), F32),
        compiler_params=_cparams(2, 32),
        name="final_norm",
    )(x, ro, *ys, g, nw)


def _rope_tables():
    t = np.arange(SEQ)
    row = (t // GRID_W).astype(np.float64)
    col = (t % GRID_W).astype(np.float64)
    half = QK_ROPE // 2
    inv = ROPE_BASE ** (-np.arange(0, half, 2, dtype=np.float64) / half)
    ang = np.concatenate([row[:, None] * inv, col[:, None] * inv], axis=-1)
    cos, sin = np.cos(ang), np.sin(ang)
    lat = np.concatenate([cos, cos, -sin, sin], axis=-1)
    ctx = np.concatenate([np.ones((CTX, QK_ROPE)), np.zeros((CTX, QK_ROPE))], axis=-1)
    tab = np.concatenate([ctx, lat], axis=0).astype(np.float32)
    return jnp.asarray(tab), jnp.asarray(tab * np.float32(QK_SCALE))


def _swap_halves(w):
    half = QK_ROPE // 2
    return jnp.concatenate([w[..., half:], w[..., :half]], axis=-1)


IN_O1 = Q_LORA + KV_LORA
IN_O2 = IN_O1 + QK_ROPE
IN_O3 = IN_O2 + 2 * CONV_CH
IN_O4 = IN_O3 + FCH
IN_COLS = IN_O4 + 3 * D
QKV_COLS = IN_O1 + 2 * QK_ROPE
PREP_TR = 256


def _prep_kernel(w_ref, qkv_ref, u_ref, f_ref, gt_ref):
    qkv_ref[:, :IN_O1] = w_ref[0, :, :IN_O1].astype(BF16)
    kb = w_ref[0, :, IN_O1:IN_O1 + 128]
    lane = lax.broadcasted_iota(jnp.int32, kb.shape, 1)
    half = QK_ROPE // 2
    swapped = jnp.where(lane < QK_ROPE, kb,
                        jnp.where(lane < QK_ROPE + half, pltpu.roll(kb, half, 1),
                                  pltpu.roll(kb, QK_ROPE + half, 1)))
    qkv_ref[:, IN_O1:] = swapped.astype(BF16)
    u_ref[...] = w_ref[0, :, IN_O2:IN_O3].astype(BF16)
    f_ref[...] = w_ref[0, :, IN_O3:IN_O4].astype(BF16)
    gt_ref[...] = w_ref[0, :, IN_O4:].astype(BF16)


def prep_w_in(w_in, l):
    widths = (QKV_COLS, 2 * CONV_CH, FCH, 3 * D)
    return pl.pallas_call(
        _prep_kernel,
        grid=(D // PREP_TR,),
        in_specs=[pl.BlockSpec((1, PREP_TR, IN_COLS), lambda i: (l, i, 0))],
        out_specs=[pl.BlockSpec((PREP_TR, n), lambda i: (i, 0)) for n in widths],
        out_shape=[jax.ShapeDtypeStruct((D, n), BF16) for n in widths],
        compiler_params=_cparams(1, 48),
        name="prep_w_in",
    )(w_in)


def _layer_weights(w_uq, w_ukv):
    wq = w_uq.reshape(Q_LORA, H, QK_NOPE + QK_ROPE)
    wq_r = wq[:, :, QK_NOPE:]
    wq = jnp.concatenate([wq[:, :, :QK_NOPE], wq_r, _swap_halves(wq_r)], axis=-1)
    wq = jnp.transpose(wq, (1, 0, 2)).astype(BF16)
    wkv = jnp.transpose(w_ukv.reshape(KV_LORA, H, QK_NOPE + V_DIM), (1, 0, 2)).astype(BF16)
    return wq, wkv


def kernel(x, c, ctx, c_ctx, w_ada, b_ada, norm1, w_in, q_norm, kv_norm, w_uq, w_ukv, w_mla_out,
           conv_dw, conv_dw_b, conv_ln_g, conv_ln_b, w_conv_out, w_four_out, w_out, norm2,
           w_router, b_router, w_gate_up, b_gate_up, w_down, b_down, norm_final):
    L = w_ada.shape[0]
    xt = jnp.concatenate([ctx, x], axis=1).reshape(T, D)
    cc = jnp.zeros((8, D), F32).at[:B].set(c).at[B].set(c_ctx)
    mod = ada_mod(cc, w_ada, b_ada).reshape(L, 8, 6, 1, D)
    tabk, tabq = _rope_tables()
    a_lat, a_ctx, cs_ch = _dft_tables()

    ro = ys = None
    g2_prev = None
    for l in range(L):
        sh1, sc1, g1, sh2, sc2, g2 = [mod[l, :, k] for k in range(6)]
        w_qkv, w_u, w_f, w_gt = prep_w_in(w_in, l)
        wq, wkv = _layer_weights(w_uq[l], w_ukv[l])
        nw1 = norm1[l].reshape(1, D)
        if l == 0:
            h = modnorm(xt, nw1, sh1, sc1)
        else:
            xt, h = modnorm_moe(xt, ro, ys, g2_prev, nw1, sh1, sc1)

        qkv = proj(_proj_plain_kernel, h, w_qkv, QKV_COLS, QKV_COLS, "proj_qkv")
        z = proj(_proj_glu_kernel, h, w_u, CONV_CH, 2 * CONV_CH, "proj_glu")
        fc, fs = proj(_proj_four_kernel, h, w_f, FCH, FCH, "proj_four", extra=(cs_ch,),
                      n_outputs=2)
        gs = proj(_proj_sigmoid_kernel, h, w_gt, 3 * D, D, "proj_gate")

        q = q_proj(qkv, q_norm[l].reshape(1, Q_LORA), wq, tabq)
        k, v = kv_proj(qkv, kv_norm[l].reshape(1, KV_LORA), wkv, tabk)
        a = attention(q, k, v)

        w_pad = jnp.zeros((32, CONV_CH), F32).at[:CONV_W].set(conv_dw[l])
        cv = conv_module(z, w_pad, conv_dw_b[l].reshape(1, CONV_CH),
                         conv_ln_g[l].reshape(1, CONV_CH), conv_ln_b[l].reshape(1, CONV_CH))
        fo = fourier(fc, fs, a_lat, a_ctx)

        m = merge1(a, cv, fo, gs, w_mla_out[l].astype(BF16), w_conv_out[l].astype(BF16),
                   w_four_out[l].astype(BF16))
        wr = jnp.zeros((D, ROUTER_PAD), BF16).at[:, :N_EXP].set(w_router[l].astype(BF16))
        xt, h2, logits = merge2(m, w_out[l].astype(BF16), xt, g1, norm2[l].reshape(1, D), sh2, sc2,
                                wr)
        last = l == L - 1
        ro, ys = moe(h2, logits, b_router[l], w_gate_up[l].astype(BF16), b_gate_up[l],
                     w_down[l].astype(BF16), b_down[l], latent_only=last)
        g2_prev = g2

    out = final_norm(xt, ro, ys, g2_prev, norm_final.reshape(1, D))
    return out.reshape(B, SEQ, D)
```

```python
import functools

import numpy as np
import jax
import jax.numpy as jnp
from jax import lax
from jax.experimental import pallas as pl
from jax.experimental.pallas import tpu as pltpu

F32 = jnp.float32
BF16 = jnp.bfloat16

D = 2048
B = 4
SEQ = 2048
CTX = 256
NB = CTX + SEQ
T = B * NB
TILE = 256
TPB = NB // TILE
NT = T // TILE
GRID_W = 64
H = 16
Q_LORA = 512
KV_LORA = 512
QK_NOPE = 128
QK_ROPE = 64
V_DIM = 128
CONV_CH = 1024
CONV_W = 31
FG = 4
FGC = 256
FCH = FG * FGC
N_EXP = 32
TOP_K = 4
D_EXP = 1024
ALPHA = 1.702
LIMIT = 7.0
EPS = 1e-6
ROPE_BASE = 10000.0
QK_SCALE = float((QK_NOPE + QK_ROPE) ** -0.5)
EXPERT_BM = 256
ROUTER_PAD = 128
GATHER_SRC_ROWS = 16384

ARB = "arbitrary"


def _cparams(n_axes, vmem_mb):
    return pltpu.CompilerParams(dimension_semantics=(ARB,) * n_axes,
                                vmem_limit_bytes=vmem_mb << 20)


def _mod_row(i):
    return jnp.where(i % TPB == 0, B, i // TPB)


def _ada_kernel(c_ref, w_ref, b_ref, o_ref):
    c = c_ref[...]
    s = (c * jax.nn.sigmoid(c)).astype(BF16)
    o_ref[0] = jnp.dot(s, w_ref[0].astype(BF16), preferred_element_type=F32) + b_ref[0]


def ada_mod(cc, w_ada, b_ada):
    L, _, N = w_ada.shape
    tn = 1024
    return pl.pallas_call(
        _ada_kernel,
        grid=(L, N // tn),
        in_specs=[pl.BlockSpec((8, D), lambda l, j: (0, 0)),
                  pl.BlockSpec((1, D, tn), lambda l, j: (l, 0, j)),
                  pl.BlockSpec((1, 1, tn), lambda l, j: (l, 0, j))],
        out_specs=pl.BlockSpec((1, 8, tn), lambda l, j: (l, 0, j)),
        out_shape=jax.ShapeDtypeStruct((L, 8, N), F32),
        compiler_params=_cparams(2, 40),
        name="ada_mod",
    )(cc, w_ada, b_ada.reshape(L, 1, N))


def _rms(x, w):
    ms = jnp.mean(x * x, axis=-1, keepdims=True)
    return x * lax.rsqrt(ms + EPS) * w


def _modnorm_kernel(x_ref, nw_ref, sh_ref, sc_ref, h_ref):
    y = _rms(x_ref[...], nw_ref[...])
    h_ref[...] = (y * (1.0 + sc_ref[0]) + sh_ref[0]).astype(BF16)


def _vec_spec():
    return pl.BlockSpec((1, 1, D), lambda i: (_mod_row(i), 0, 0))


def _row_spec(cols=D):
    return pl.BlockSpec((TILE, cols), lambda i: (i, 0))


def _const_spec(shape):
    return pl.BlockSpec(shape, lambda i: (0,) * len(shape))


def modnorm(x, nw, sh, sc):
    return pl.pallas_call(
        _modnorm_kernel,
        grid=(NT,),
        in_specs=[_row_spec(), _const_spec((1, D)), _vec_spec(), _vec_spec()],
        out_specs=_row_spec(),
        out_shape=jax.ShapeDtypeStruct((T, D), BF16),
        compiler_params=_cparams(1, 32),
        name="modnorm",
    )(x, nw, sh, sc)


PROJ_TM = 768


def _proj_plain_kernel(a_ref, w_ref, o_ref):
    o_ref[...] = jnp.dot(a_ref[...], w_ref[...], preferred_element_type=F32).astype(o_ref.dtype)


def _proj_sigmoid_kernel(a_ref, w_ref, o_ref):
    acc = jnp.dot(a_ref[...], w_ref[...], preferred_element_type=F32)
    o_ref[...] = jax.nn.sigmoid(acc).astype(o_ref.dtype)


def _proj_glu_kernel(a_ref, w_ref, o_ref):
    acc = jnp.dot(a_ref[...], w_ref[...], preferred_element_type=F32)
    o_ref[...] = (acc[:, :CONV_CH] * jax.nn.sigmoid(acc[:, CONV_CH:])).astype(o_ref.dtype)


def _proj_four_kernel(a_ref, w_ref, cs_ref, fc_ref, fs_ref):
    f = jnp.dot(a_ref[...], w_ref[...], preferred_element_type=F32).astype(BF16)
    cs = cs_ref[...].astype(BF16)
    for g in range(FG):
        r = jnp.dot(f[:, g * FGC:(g + 1) * FGC], cs, preferred_element_type=F32)
        fc_ref[:, g * FGC:(g + 1) * FGC] = r[:, :FGC].astype(BF16)
        fs_ref[:, g * FGC:(g + 1) * FGC] = r[:, FGC:].astype(BF16)


def proj(kernel, h, w, n_out, tn, name, extra=(), n_outputs=1, vmem_mb=48):
    K, N = w.shape
    tm = PROJ_TM
    tn_out = n_out // (N // tn)
    in_specs = [pl.BlockSpec((tm, K), lambda j, i: (i, 0)),
                pl.BlockSpec((K, tn), lambda j, i: (0, j))]
    for e in extra:
        in_specs.append(pl.BlockSpec(e.shape, lambda j, i, nd=e.ndim: (0,) * nd))
    out_spec = pl.BlockSpec((tm, tn_out), lambda j, i: (i, j))
    out_shape = jax.ShapeDtypeStruct((T, n_out), BF16)
    if n_outputs > 1:
        out_spec = [out_spec] * n_outputs
        out_shape = [out_shape] * n_outputs
    return pl.pallas_call(
        kernel,
        grid=(N // tn, T // tm),
        in_specs=in_specs,
        out_specs=out_spec,
        out_shape=out_shape,
        compiler_params=_cparams(2, vmem_mb),
        name=name,
    )(h, w, *extra)


MLA_TM = 768
MLA_HG = 4


def _lane_lt64(shape):
    return lax.broadcasted_iota(jnp.int32, shape, 1) < QK_ROPE


def _qproj_kernel(cq_ref, nw_ref, w_ref, tab_ref, q_ref, cqn_ref):
    @pl.when(pl.program_id(1) == 0)
    def _():
        cqn_ref[...] = _rms(cq_ref[...].astype(F32), nw_ref[...]).astype(BF16)

    for hh in range(MLA_HG):
        y = jnp.dot(cqn_ref[...], w_ref[hh], preferred_element_type=F32)
        a = y[:, QK_NOPE:] * tab_ref[...]
        q_ref[hh, :, :QK_NOPE] = (y[:, :QK_NOPE] * QK_SCALE).astype(BF16)
        q_ref[hh, :, QK_NOPE:] = (a + pltpu.roll(a, QK_ROPE, 1)).astype(BF16)


def q_proj(qkv, nw, wq, tabq):
    tm = MLA_TM
    return pl.pallas_call(
        _qproj_kernel,
        grid=(T // tm, H // MLA_HG),
        in_specs=[pl.BlockSpec((tm, Q_LORA), lambda i, h: (i, 0)),
                  pl.BlockSpec((1, Q_LORA), lambda i, h: (0, 0)),
                  pl.BlockSpec((MLA_HG, Q_LORA, 256), lambda i, h: (h, 0, 0)),
                  pl.BlockSpec((tm, 128), lambda i, h: (i % (NB // tm), 0))],
        out_specs=pl.BlockSpec((MLA_HG, tm, 256), lambda i, h: (h, i, 0)),
        out_shape=jax.ShapeDtypeStruct((H, T, 256), BF16),
        scratch_shapes=[pltpu.VMEM((tm, Q_LORA), BF16)],
        compiler_params=_cparams(2, 32),
        name="q_proj",
    )(qkv, nw, wq, tabq)


def _kvproj_kernel(ckv_ref, kr_ref, nw_ref, w_ref, tab_ref, k_ref, v_ref, ckvn_ref, k2_ref):
    @pl.when(pl.program_id(1) == 0)
    def _():
        ckvn_ref[...] = _rms(ckv_ref[...].astype(F32), nw_ref[...]).astype(BF16)
        a = kr_ref[...].astype(F32) * tab_ref[...]
        s = a + pltpu.roll(a, QK_ROPE, 1)
        k2_ref[...] = jnp.where(_lane_lt64(s.shape), s, 0.0).astype(BF16)

    for hh in range(MLA_HG):
        y = jnp.dot(ckvn_ref[...], w_ref[hh], preferred_element_type=F32)
        k_ref[hh, :, :QK_NOPE] = y[:, :QK_NOPE].astype(BF16)
        k_ref[hh, :, QK_NOPE:] = k2_ref[...]
        v_ref[hh] = y[:, QK_NOPE:].astype(BF16)


def kv_proj(qkv, nw, wkv, tabk):
    tm = MLA_TM
    return pl.pallas_call(
        _kvproj_kernel,
        grid=(T // tm, H // MLA_HG),
        in_specs=[pl.BlockSpec((tm, KV_LORA), lambda i, h: (i, 1)),
                  pl.BlockSpec((tm, 128), lambda i, h: (i, (Q_LORA + KV_LORA) // 128)),
                  pl.BlockSpec((1, KV_LORA), lambda i, h: (0, 0)),
                  pl.BlockSpec((MLA_HG, KV_LORA, 256), lambda i, h: (h, 0, 0)),
                  pl.BlockSpec((tm, 128), lambda i, h: (i % (NB // tm), 0))],
        out_specs=[pl.BlockSpec((MLA_HG, tm, 256), lambda i, h: (h, i, 0)),
                   pl.BlockSpec((MLA_HG, tm, V_DIM), lambda i, h: (h, i, 0))],
        out_shape=[jax.ShapeDtypeStruct((H, T, 256), BF16),
                   jax.ShapeDtypeStruct((H, T, V_DIM), BF16)],
        scratch_shapes=[pltpu.VMEM((tm, KV_LORA), BF16), pltpu.VMEM((tm, 128), BF16)],
        compiler_params=_cparams(2, 32),
        name="kv_proj",
    )(qkv, qkv, nw, wkv, tabk)


ATT_TQ = 256


def _attend(q, k, v):
    s = lax.dot_general(q, k, (((1,), (1,)), ((), ())), preferred_element_type=F32)
    m = jnp.max(s, axis=-1, keepdims=True)
    p = jnp.exp(s - m)
    l = jnp.sum(p, axis=-1, keepdims=True)
    o = jnp.dot(p.astype(BF16), v, preferred_element_type=F32)
    return (o / l).astype(BF16)


def _attn_kernel(q_ref, k_ref, v_ref, o_ref):
    o_ref[:CTX, :] = _attend(q_ref[0, :CTX, :], k_ref[0, :CTX, :], v_ref[0, :CTX, :])
    for c in range(SEQ // ATT_TQ):
        r0 = CTX + c * ATT_TQ
        o_ref[r0:r0 + ATT_TQ, :] = _attend(q_ref[0, r0:r0 + ATT_TQ, :], k_ref[0], v_ref[0])


def attention(q, k, v):
    return pl.pallas_call(
        _attn_kernel,
        grid=(B, H),
        in_specs=[pl.BlockSpec((1, NB, 256), lambda b, h: (h, b, 0)),
                  pl.BlockSpec((1, NB, 256), lambda b, h: (h, b, 0)),
                  pl.BlockSpec((1, NB, V_DIM), lambda b, h: (h, b, 0))],
        out_specs=pl.BlockSpec((NB, V_DIM), lambda b, h: (b, h)),
        out_shape=jax.ShapeDtypeStruct((T, H * V_DIM), BF16),
        compiler_params=_cparams(2, 48),
        name="attention",
    )(q, k, v)


HALO = 16
CONV_RC = 32
SUBLANES = 8
CONV_SH_ROWS = TILE + 2 * HALO - SUBLANES


def _conv_kernel(zp_ref, zc_ref, zn_ref, w_ref, b_ref, g_ref, be_ref, o_ref, buf_ref, sh_ref):
    j = pl.program_id(0) % TPB
    prev_ok = j >= 2
    next_ok = jnp.logical_and(j >= 1, j <= TPB - 2)
    buf_ref[0:HALO, :] = jnp.where(prev_ok, zp_ref[...].astype(F32), 0.0)
    buf_ref[HALO:HALO + TILE, :] = zc_ref[...].astype(F32)
    buf_ref[HALO + TILE:, :] = jnp.where(next_ok, zn_ref[...].astype(F32), 0.0)
    for s in range(1, SUBLANES):
        sh_ref[s - 1] = buf_ref[s:s + CONV_SH_ROWS, :]
    off = HALO - CONV_W // 2
    for rc in range(TILE // CONV_RC):
        r0 = rc * CONV_RC
        acc = jnp.zeros((CONV_RC, CONV_CH), F32) + b_ref[...]
        for t in range(CONV_W):
            q, s = divmod(off + t, SUBLANES)
            a0 = r0 + q * SUBLANES
            if s == 0:
                tap = buf_ref[a0:a0 + CONV_RC, :]
            else:
                tap = sh_ref[s - 1, a0:a0 + CONV_RC, :]
            acc = acc + tap * w_ref[t:t + 1, :]
        mu = jnp.mean(acc, axis=-1, keepdims=True)
        d = acc - mu
        var = jnp.mean(d * d, axis=-1, keepdims=True)
        zn = d * lax.rsqrt(var + EPS) * g_ref[...] + be_ref[...]
        o_ref[r0:r0 + CONV_RC, :] = (zn * jax.nn.sigmoid(zn)).astype(BF16)


def conv_module(z, w_pad, b, g, be):
    nh = T // HALO
    return pl.pallas_call(
        _conv_kernel,
        grid=(NT,),
        in_specs=[pl.BlockSpec((HALO, CONV_CH),
                               lambda i: (jnp.maximum(i * (TILE // HALO) - 1, 0), 0)),
                  pl.BlockSpec((TILE, CONV_CH), lambda i: (i, 0)),
                  pl.BlockSpec((HALO, CONV_CH),
                               lambda i: (jnp.minimum((i + 1) * (TILE // HALO), nh - 1), 0)),
                  _const_spec((32, CONV_CH)), _const_spec((1, CONV_CH)),
                  _const_spec((1, CONV_CH)), _const_spec((1, CONV_CH))],
        out_specs=pl.BlockSpec((TILE, CONV_CH), lambda i: (i, 0)),
        out_shape=jax.ShapeDtypeStruct((T, CONV_CH), BF16),
        scratch_shapes=[pltpu.VMEM((TILE + 2 * HALO, CONV_CH), F32),
                        pltpu.VMEM((SUBLANES - 1, CONV_SH_ROWS, CONV_CH), F32)],
        compiler_params=_cparams(1, 40),
        name="conv_module",
    )(z, z, z, w_pad, b, g, be)


def _dft_tables():
    def cs(n):
        k = np.arange(n, dtype=np.int64)
        ang = 2.0 * np.pi * ((k[:, None] * k[None, :]) % n).astype(np.float64) / n
        return np.cos(ang), np.sin(ang)

    cl, sl = cs(SEQ)
    cc, sc = cs(CTX)
    cg, sg = cs(FGC)
    a_lat = np.concatenate([cl, -sl], axis=1).astype(np.float32)
    a_ctx = np.concatenate([cc, -sc], axis=1).astype(np.float32)
    cs_ch = np.concatenate([cg, sg], axis=1).astype(np.float32)
    return jnp.asarray(a_lat), jnp.asarray(a_ctx), jnp.asarray(cs_ch)


LAT_SCALE = float((SEQ * FGC) ** -0.5)
CTX_SCALE = float((CTX * FGC) ** -0.5)


def _fourier_kernel(al_ref, ac_ref, fc_ref, fs_ref, o_ref):
    m = pl.program_id(1)

    @pl.when(m == 0)
    def _():
        r = jnp.dot(ac_ref[:, :CTX].astype(BF16), fc_ref[0, :CTX, :], preferred_element_type=F32)
        r = r + jnp.dot(ac_ref[:, CTX:].astype(BF16), fs_ref[0, :CTX, :],
                        preferred_element_type=F32)
        o_ref[...] = (r * CTX_SCALE).astype(BF16)

    @pl.when(m > 0)
    def _():
        r = jnp.dot(al_ref[:, :SEQ].astype(BF16), fc_ref[0, CTX:, :], preferred_element_type=F32)
        r = r + jnp.dot(al_ref[:, SEQ:].astype(BF16), fs_ref[0, CTX:, :],
                        preferred_element_type=F32)
        o_ref[...] = (r * LAT_SCALE).astype(BF16)


def fourier(fc, fs, a_lat, a_ctx):
    return pl.pallas_call(
        _fourier_kernel,
        grid=(B, TPB),
        in_specs=[pl.BlockSpec((TILE, 2 * SEQ), lambda b, m: (jnp.maximum(m - 1, 0), 0)),
                  pl.BlockSpec((CTX, 2 * CTX), lambda b, m: (0, 0)),
                  pl.BlockSpec((1, NB, FCH), lambda b, m: (b, 0, 0)),
                  pl.BlockSpec((1, NB, FCH), lambda b, m: (b, 0, 0))],
        out_specs=pl.BlockSpec((TILE, FCH), lambda b, m: (b * TPB + m, 0)),
        out_shape=jax.ShapeDtypeStruct((T, FCH), BF16),
        compiler_params=_cparams(2, 48),
        name="fourier",
    )(a_lat, a_ctx, fc.reshape(B, NB, FCH), fs.reshape(B, NB, FCH))


MERGE_TM = 512
MERGE_TN = 1024


def _merge1_kernel(a_ref, cv_ref, fo_ref, g0_ref, g1_ref, g2_ref, wm_ref, wc_ref, wf_ref, o_ref):
    m = g0_ref[...].astype(F32) * jnp.dot(a_ref[...], wm_ref[...], preferred_element_type=F32)
    m = m + g1_ref[...].astype(F32) * jnp.dot(cv_ref[...], wc_ref[...],
                                               preferred_element_type=F32)
    m = m + g2_ref[...].astype(F32) * jnp.dot(fo_ref[...], wf_ref[...],
                                               preferred_element_type=F32)
    o_ref[...] = m.astype(BF16)


def merge1(a, cv, fo, gs, wm, wc, wf):
    tm, tn = MERGE_TM, MERGE_TN
    nj = D // tn

    def gspec(k):
        return pl.BlockSpec((tm, tn), lambda j, i: (i, k * nj + j))

    return pl.pallas_call(
        _merge1_kernel,
        grid=(nj, T // tm),
        in_specs=[pl.BlockSpec((tm, H * V_DIM), lambda j, i: (i, 0)),
                  pl.BlockSpec((tm, CONV_CH), lambda j, i: (i, 0)),
                  pl.BlockSpec((tm, FCH), lambda j, i: (i, 0)),
                  gspec(0), gspec(1), gspec(2),
                  pl.BlockSpec((H * V_DIM, tn), lambda j, i: (0, j)),
                  pl.BlockSpec((CONV_CH, tn), lambda j, i: (0, j)),
                  pl.BlockSpec((FCH, tn), lambda j, i: (0, j))],
        out_specs=pl.BlockSpec((tm, tn), lambda j, i: (i, j)),
        out_shape=jax.ShapeDtypeStruct((T, D), BF16),
        compiler_params=_cparams(2, 48),
        name="merge1",
    )(a, cv, fo, gs, gs, gs, wm, wc, wf)


def _merge2_kernel(m_ref, w_ref, x_ref, g_ref, nw_ref, sh_ref, sc_ref, wr_ref,
                   xo_ref, h_ref, lg_ref):
    acc = jnp.dot(m_ref[...], w_ref[...], preferred_element_type=F32)
    x = x_ref[...] + g_ref[0] * acc
    xo_ref[...] = x
    h = (_rms(x, nw_ref[...]) * (1.0 + sc_ref[0]) + sh_ref[0]).astype(BF16)
    h_ref[...] = h
    lg_ref[...] = jnp.dot(h, wr_ref[...], preferred_element_type=F32)


def merge2(m, w_out, x, g, nw, sh, sc, wr):
    return pl.pallas_call(
        _merge2_kernel,
        grid=(NT,),
        in_specs=[_row_spec(), _const_spec((D, D)), _row_spec(), _vec_spec(),
                  _const_spec((1, D)), _vec_spec(), _vec_spec(), _const_spec((D, ROUTER_PAD))],
        out_specs=[_row_spec(), _row_spec(), _row_spec(ROUTER_PAD)],
        out_shape=[jax.ShapeDtypeStruct((T, D), F32), jax.ShapeDtypeStruct((T, D), BF16),
                   jax.ShapeDtypeStruct((T, ROUTER_PAD), F32)],
        compiler_params=_cparams(1, 48),
        name="merge2",
    )(m, w_out, x, g, nw, sh, sc, wr)


W_PARTS = 4
GU_ROWS = D // W_PARTS
DN_ROWS = D_EXP // W_PARTS
N_STAGE = 2
EXPERT_RANGES = 4


def _expert_kernel(be_ref, slot_ref, nxt_ref, first_ref, c0_ref, nch_ref, nv_ref,
                   x_ref, bgu_ref, bd_ref, wgu_hbm, wd_hbm, y_buf_hbm, o_ref,
                   wgu_buf, wd_buf, st_gu, st_d, sem_gu, sem_d, *, layer):
    del y_buf_hbm
    i = pl.program_id(0)

    def part_copies(e, c, b):
        gu = pltpu.make_async_copy(wgu_hbm.at[layer, e, pl.ds(c * GU_ROWS, GU_ROWS), :],
                                   st_gu.at[b], sem_gu.at[b])
        dn = pltpu.make_async_copy(wd_hbm.at[layer, e, pl.ds(c * DN_ROWS, DN_ROWS), :],
                                   st_d.at[b], sem_d.at[b])
        return gu, dn

    def start_part(e, c, b):
        gu, dn = part_copies(e, c, b)
        gu.start()
        dn.start()

    def take_part(e, c, dst):
        b = lax.rem(c, N_STAGE)
        gu, dn = part_copies(e, c, b)
        gu.wait()
        dn.wait()
        wgu_buf[dst, pl.ds(pl.multiple_of(c * GU_ROWS, GU_ROWS), GU_ROWS), :] = (
            st_gu[b].astype(BF16))
        wd_buf[dst, pl.ds(pl.multiple_of(c * DN_ROWS, DN_ROWS), DN_ROWS), :] = (
            st_d[b].astype(BF16))

        @pl.when(c + N_STAGE < W_PARTS)
        def _():
            start_part(e, c + N_STAGE, b)

    @pl.when(i < nv_ref[0])
    def _():
        e = be_ref[i]
        s = slot_ref[i]
        nx = nxt_ref[i]

        @pl.when(i == 0)
        def _():
            for b in range(N_STAGE):
                start_part(e, b, b)
            for c in range(W_PARTS):
                take_part(e, jnp.int32(c), s)

        @pl.when(jnp.logical_and(first_ref[i] == 1, nx >= 0))
        def _():
            for b in range(N_STAGE):
                start_part(nx, b, b)

        hgu = jnp.dot(x_ref[...], wgu_buf[s], preferred_element_type=F32) + bgu_ref[0]
        glu = jnp.minimum(hgu[:, :D_EXP], LIMIT)
        lin = jnp.clip(hgu[:, D_EXP:], -LIMIT, LIMIT)
        act = glu * jax.nn.sigmoid(ALPHA * glu) * (lin + 1.0)
        y = jnp.dot(act.astype(BF16), wd_buf[s], preferred_element_type=F32) + bd_ref[0]
        o_ref[...] = y.astype(BF16)

        def body(k, carry):
            take_part(nx, c0_ref[i] + k, 1 - s)
            return carry

        lax.fori_loop(0, nch_ref[i], body, 0)

    @pl.when(i >= nv_ref[0])
    def _():
        o_ref[...] = jnp.zeros(o_ref.shape, BF16)


def _expert_plan(block_e, n_valid, nblk):
    idx = jnp.arange(nblk, dtype=jnp.int32)
    valid = idx < n_valid
    prev_e = jnp.concatenate([block_e[:1], block_e[:-1]])
    first = jnp.logical_and(valid, jnp.logical_or(idx == 0, block_e != prev_e))
    run_id = jnp.cumsum(first.astype(jnp.int32)) - 1
    n_runs = jnp.sum(first.astype(jnp.int32))
    member = jnp.logical_and(run_id[None, :] == idx[:, None], valid[None, :])
    run_len = jnp.sum(member.astype(jnp.int32), axis=1)
    run_first = jnp.min(jnp.where(member, idx[None, :], nblk), axis=1)
    run_e = block_e[jnp.minimum(run_first, nblk - 1)]
    rlen = jnp.maximum(run_len[run_id], 1)
    j = idx - run_first[run_id]
    has_next = jnp.logical_and(valid, run_id + 1 < n_runs)
    nxt = jnp.where(has_next, run_e[jnp.minimum(run_id + 1, nblk - 1)], -1)
    c_lo = (W_PARTS * j) // rlen
    c_hi = (W_PARTS * (j + 1)) // rlen
    nch = jnp.where(has_next, c_hi - c_lo, 0)
    slot = run_id % 2
    cast = lambda a: a.astype(jnp.int32)
    return cast(slot), cast(nxt), cast(first), cast(c_lo), cast(nch)


def experts(block_e, n_valid, xb, wgu, bgu, wd, bd, layer, y_buf, blk0):
    nblk = xb.shape[0] // EXPERT_BM
    slot, nxt, first, c0, nch = _expert_plan(block_e, n_valid[0], nblk)
    n_prefetch = 7

    def xmap(i, *s):
        return (jnp.maximum(jnp.minimum(i, s[n_prefetch - 1][0] - 1), 0), 0)

    def bmap(i, *s):
        return (layer * N_EXP + s[0][i], 0, 0)

    any_spec = pl.BlockSpec(memory_space=pl.ANY)
    y_buf_operand = n_prefetch + 5

    grid_spec = pltpu.PrefetchScalarGridSpec(
        num_scalar_prefetch=n_prefetch,
        grid=(nblk,),
        in_specs=[pl.BlockSpec((EXPERT_BM, D), xmap),
                  pl.BlockSpec((1, 1, 2 * D_EXP), bmap),
                  pl.BlockSpec((1, 1, D), bmap),
                  any_spec, any_spec, any_spec],
        out_specs=pl.BlockSpec((EXPERT_BM, D), lambda i, *s: (blk0 + i, 0)),
        scratch_shapes=[pltpu.VMEM((2, D, 2 * D_EXP), BF16),
                        pltpu.VMEM((2, D_EXP, D), BF16),
                        pltpu.VMEM((N_STAGE, GU_ROWS, 2 * D_EXP), F32),
                        pltpu.VMEM((N_STAGE, DN_ROWS, D), F32),
                        pltpu.SemaphoreType.DMA((N_STAGE,)),
                        pltpu.SemaphoreType.DMA((N_STAGE,))],
    )
    L = wgu.shape[0]
    return pl.pallas_call(
        functools.partial(_expert_kernel, layer=layer),
        grid_spec=grid_spec,
        out_shape=jax.ShapeDtypeStruct(y_buf.shape, BF16),
        input_output_aliases={y_buf_operand: 0},
        compiler_params=_cparams(1, 56),
        name="experts",
    )(block_e, slot, nxt, first, c0, nch, n_valid, xb,
      bgu.reshape(L * N_EXP, 1, 2 * D_EXP), bd.reshape(L * N_EXP, 1, D), wgu, wd, y_buf)


NEG_BIG = -1e30
RO_E, RO_RANK, RO_GATE = 0, TOP_K, 2 * TOP_K


def _router_kernel(lg_ref, b_ref, tri_ref, ro_ref, cnt_ref, base_ref):
    @pl.when(pl.program_id(0) == 0)
    def _():
        base_ref[...] = jnp.zeros(base_ref.shape, F32)

    lane = lax.broadcasted_iota(jnp.int32, (TILE, ROUTER_PAD), 1)
    lanef = lane.astype(F32)
    lg = jnp.where(lane < N_EXP, lg_ref[...] + b_ref[...], NEG_BIG)
    vals, idxs, hots = [], [], []
    for _ in range(TOP_K):
        m = jnp.max(lg, axis=-1, keepdims=True)
        idx = jnp.min(jnp.where(lg == m, lanef, float(ROUTER_PAD)), axis=-1, keepdims=True)
        hot = lanef == idx
        lg = jnp.where(hot, NEG_BIG, lg)
        vals.append(m)
        idxs.append(idx)
        hots.append(hot)
    ex = [jnp.exp(v - vals[0]) for v in vals]
    den = ex[0] + ex[1] + ex[2] + ex[3]
    chosen = jnp.zeros((TILE, ROUTER_PAD), F32)
    for hot in hots:
        chosen = jnp.where(hot, 1.0, chosen)
    before = jnp.dot(tri_ref[...], chosen.astype(BF16), preferred_element_type=F32)
    tot = before + base_ref[0:1, :]
    rec = jnp.zeros((TILE, ROUTER_PAD), F32)
    for k in range(TOP_K):
        rank = jnp.sum(jnp.where(hots[k], tot, 0.0), axis=-1, keepdims=True)
        rec = jnp.where(lane == RO_E + k, idxs[k], rec)
        rec = jnp.where(lane == RO_RANK + k, rank, rec)
        rec = jnp.where(lane == RO_GATE + k, ex[k] / den, rec)
    ro_ref[...] = rec
    base_ref[...] = base_ref[...] + jnp.sum(chosen, axis=0, keepdims=True)
    cnt_ref[...] = base_ref[...]


def router(logits, b_pad, tri, latent_only):
    if latent_only:
        ntl = SEQ // TILE
        n_tiles = B * ntl
        in_map = lambda i: ((i // ntl) * TPB + 1 + i % ntl, 0)
    else:
        n_tiles = NT
        in_map = lambda i: (i, 0)
    return pl.pallas_call(
        _router_kernel,
        grid=(n_tiles,),
        in_specs=[pl.BlockSpec((TILE, ROUTER_PAD), in_map),
                  _const_spec((1, ROUTER_PAD)), _const_spec((TILE, TILE))],
        out_specs=[_row_spec(ROUTER_PAD), _const_spec((8, ROUTER_PAD))],
        out_shape=[jax.ShapeDtypeStruct((n_tiles * TILE, ROUTER_PAD), F32),
                   jax.ShapeDtypeStruct((8, ROUTER_PAD), F32)],
        scratch_shapes=[pltpu.VMEM((8, ROUTER_PAD), F32)],
        compiler_params=_cparams(1, 32),
        name="router",
    )(logits, b_pad, tri)


def moe(h2, logits, b_router, wgu, bgu, wd, bd, layer, latent_only, y_buf):
    b_pad = jnp.zeros((1, ROUTER_PAD), F32).at[0, :N_EXP].set(b_router)
    tri = jnp.asarray(np.tril(np.ones((TILE, TILE), np.float32), -1), dtype=BF16)
    ro, cnt = router(logits, b_pad, tri, latent_only)
    n_tok = ro.shape[0]
    if latent_only:
        tok_ids = (jnp.arange(B, dtype=jnp.int32)[:, None] * NB + CTX
                   + jnp.arange(SEQ, dtype=jnp.int32)[None, :]).reshape(-1)
    else:
        tok_ids = jnp.arange(T, dtype=jnp.int32)
    e = ro[:, RO_E:RO_E + TOP_K].astype(jnp.int32)
    rank = ro[:, RO_RANK:RO_RANK + TOP_K].astype(jnp.int32)
    counts = cnt[0, :N_EXP].astype(jnp.int32)
    padded = (counts + EXPERT_BM - 1) // EXPERT_BM * EXPERT_BM
    pends = jnp.cumsum(padded)
    pstarts = pends - padded
    dest = pstarts[e] + rank
    n_pair = n_tok * TOP_K
    nblk = n_pair // EXPERT_BM + N_EXP
    n_rows = nblk * EXPERT_BM
    tok_buf = jnp.zeros((n_rows,), jnp.int32).at[dest.reshape(-1)].set(jnp.repeat(tok_ids, TOP_K))
    n_valid = (pends[-1] // EXPERT_BM).astype(jnp.int32)
    blk = jnp.arange(nblk, dtype=jnp.int32)
    block_e = jnp.minimum(jnp.sum((pends[None, :] <= blk[:, None] * EXPERT_BM).astype(jnp.int32),
                                  axis=1), N_EXP - 1)
    block_e = jnp.where(blk < n_valid, block_e, block_e[n_valid - 1])
    h2_src = jnp.concatenate([h2, jnp.zeros((GATHER_SRC_ROWS - T, D), BF16)], axis=0)
    rb = nblk // EXPERT_RANGES
    yb = y_buf
    for r in range(EXPERT_RANGES):
        xb = h2_src[tok_buf[r * rb * EXPERT_BM:(r + 1) * rb * EXPERT_BM]]
        nv_r = jnp.clip(n_valid - r * rb, 0, rb).astype(jnp.int32).reshape(1)
        yb = experts(block_e[r * rb:(r + 1) * rb], nv_r, xb, wgu, bgu, wd, bd, layer,
                     yb, r * rb)
    ys = [yb[dest[:, k]] for k in range(TOP_K)]
    return ro, ys, yb


def _combine(ro_ref, y_refs):
    ro = ro_ref[...]
    acc = ro[:, RO_GATE:RO_GATE + 1] * y_refs[0][...].astype(F32)
    for k in range(1, TOP_K):
        acc = acc + ro[:, RO_GATE + k:RO_GATE + k + 1] * y_refs[k][...].astype(F32)
    return acc


def _modnorm_moe_kernel(x_ref, ro_ref, y0_ref, y1_ref, y2_ref, y3_ref, g_ref, nw_ref, sh_ref,
                        sc_ref, xo_ref, h_ref):
    x = x_ref[...] + g_ref[0] * _combine(ro_ref, (y0_ref, y1_ref, y2_ref, y3_ref))
    xo_ref[...] = x
    y = _rms(x, nw_ref[...])
    h_ref[...] = (y * (1.0 + sc_ref[0]) + sh_ref[0]).astype(BF16)


def modnorm_moe(x, ro, ys, g, nw, sh, sc):
    return pl.pallas_call(
        _modnorm_moe_kernel,
        grid=(NT,),
        in_specs=[_row_spec(), _row_spec(ROUTER_PAD)] + [_row_spec()] * TOP_K
                 + [_vec_spec(), _const_spec((1, D)), _vec_spec(), _vec_spec()],
        out_specs=[_row_spec(), _row_spec()],
        out_shape=[jax.ShapeDtypeStruct((T, D), F32), jax.ShapeDtypeStruct((T, D), BF16)],
        compiler_params=_cparams(1, 40),
        name="modnorm_moe",
    )(x, ro, *ys, g, nw, sh, sc)


def _final_kernel(x_ref, ro_ref, y0_ref, y1_ref, y2_ref, y3_ref, g_ref, nw_ref, o_ref):
    x = x_ref[...] + g_ref[0] * _combine(ro_ref, (y0_ref, y1_ref, y2_ref, y3_ref))
    o_ref[...] = _rms(x, nw_ref[...])


def final_norm(x, ro, ys, g, nw):
    ntl = SEQ // TILE
    lat = lambda cols: pl.BlockSpec((TILE, cols), lambda b, m: (b * ntl + m, 0))
    return pl.pallas_call(
        _final_kernel,
        grid=(B, ntl),
        in_specs=[pl.BlockSpec((TILE, D), lambda b, m: (b * TPB + 1 + m, 0)),
                  lat(ROUTER_PAD)] + [lat(D)] * TOP_K
                 + [pl.BlockSpec((1, 1, D), lambda b, m: (b, 0, 0)),
                    pl.BlockSpec((1, D), lambda b, m: (0, 0))],
        out_specs=lat(D),
        out_shape=jax.ShapeDtypeStruct((B * SEQ, D), F32),
        compiler_params=_cparams(2, 32),
        name="final_norm",
    )(x, ro, *ys, g, nw)


def _rope_tables():
    t = np.arange(SEQ)
    row = (t // GRID_W).astype(np.float64)
    col = (t % GRID_W).astype(np.float64)
    half = QK_ROPE // 2
    inv = ROPE_BASE ** (-np.arange(0, half, 2, dtype=np.float64) / half)
    ang = np.concatenate([row[:, None] * inv, col[:, None] * inv], axis=-1)
    cos, sin = np.cos(ang), np.sin(ang)
    lat = np.concatenate([cos, cos, -sin, sin], axis=-1)
    ctx = np.concatenate([np.ones((CTX, QK_ROPE)), np.zeros((CTX, QK_ROPE))], axis=-1)
    tab = np.concatenate([ctx, lat], axis=0).astype(np.float32)
    return jnp.asarray(tab), jnp.asarray(tab * np.float32(QK_SCALE))


def _swap_halves(w):
    half = QK_ROPE // 2
    return jnp.concatenate([w[..., half:], w[..., :half]], axis=-1)


IN_O1 = Q_LORA + KV_LORA
IN_O2 = IN_O1 + QK_ROPE
IN_O3 = IN_O2 + 2 * CONV_CH
IN_O4 = IN_O3 + FCH
IN_COLS = IN_O4 + 3 * D
QKV_COLS = IN_O1 + 2 * QK_ROPE
PREP_TR = 256


def _prep_kernel(w_ref, qkv_ref, u_ref, f_ref, gt_ref):
    qkv_ref[:, :IN_O1] = w_ref[0, :, :IN_O1].astype(BF16)
    kb = w_ref[0, :, IN_O1:IN_O1 + 128]
    lane = lax.broadcasted_iota(jnp.int32, kb.shape, 1)
    half = QK_ROPE // 2
    swapped = jnp.where(lane < QK_ROPE, kb,
                        jnp.where(lane < QK_ROPE + half, pltpu.roll(kb, half, 1),
                                  pltpu.roll(kb, QK_ROPE + half, 1)))
    qkv_ref[:, IN_O1:] = swapped.astype(BF16)
    u_ref[...] = w_ref[0, :, IN_O2:IN_O3].astype(BF16)
    f_ref[...] = w_ref[0, :, IN_O3:IN_O4].astype(BF16)
    gt_ref[...] = w_ref[0, :, IN_O4:].astype(BF16)


def prep_w_in(w_in, l):
    widths = (QKV_COLS, 2 * CONV_CH, FCH, 3 * D)
    return pl.pallas_call(
        _prep_kernel,
        grid=(D // PREP_TR,),
        in_specs=[pl.BlockSpec((1, PREP_TR, IN_COLS), lambda i: (l, i, 0))],
        out_specs=[pl.BlockSpec((PREP_TR, n), lambda i: (i, 0)) for n in widths],
        out_shape=[jax.ShapeDtypeStruct((D, n), BF16) for n in widths],
        compiler_params=_cparams(1, 48),
        name="prep_w_in",
    )(w_in)


def _layer_weights(w_uq, w_ukv):
    wq = w_uq.reshape(Q_LORA, H, QK_NOPE + QK_ROPE)
    wq_r = wq[:, :, QK_NOPE:]
    wq = jnp.concatenate([wq[:, :, :QK_NOPE], wq_r, _swap_halves(wq_r)], axis=-1)
    wq = jnp.transpose(wq, (1, 0, 2)).astype(BF16)
    wkv = jnp.transpose(w_ukv.reshape(KV_LORA, H, QK_NOPE + V_DIM), (1, 0, 2)).astype(BF16)
    return wq, wkv


def kernel(x, c, ctx, c_ctx, w_ada, b_ada, norm1, w_in, q_norm, kv_norm, w_uq, w_ukv, w_mla_out,
           conv_dw, conv_dw_b, conv_ln_g, conv_ln_b, w_conv_out, w_four_out, w_out, norm2,
           w_router, b_router, w_gate_up, b_gate_up, w_down, b_down, norm_final):
    L = w_ada.shape[0]
    xt = jnp.concatenate([ctx, x], axis=1).reshape(T, D)
    cc = jnp.zeros((8, D), F32).at[:B].set(c).at[B].set(c_ctx)
    mod = ada_mod(cc, w_ada, b_ada).reshape(L, 8, 6, 1, D)
    tabk, tabq = _rope_tables()
    a_lat, a_ctx, cs_ch = _dft_tables()

    ro = ys = None
    g2_prev = None
    y_buf = jnp.zeros(((T * TOP_K // EXPERT_BM + N_EXP) * EXPERT_BM, D), BF16)
    for l in range(L):
        sh1, sc1, g1, sh2, sc2, g2 = [mod[l, :, k] for k in range(6)]
        w_qkv, w_u, w_f, w_gt = prep_w_in(w_in, l)
        wq, wkv = _layer_weights(w_uq[l], w_ukv[l])
        nw1 = norm1[l].reshape(1, D)
        if l == 0:
            h = modnorm(xt, nw1, sh1, sc1)
        else:
            xt, h = modnorm_moe(xt, ro, ys, g2_prev, nw1, sh1, sc1)

        qkv = proj(_proj_plain_kernel, h, w_qkv, QKV_COLS, QKV_COLS, "proj_qkv")
        z = proj(_proj_glu_kernel, h, w_u, CONV_CH, 2 * CONV_CH, "proj_glu")
        fc, fs = proj(_proj_four_kernel, h, w_f, FCH, FCH, "proj_four", extra=(cs_ch,),
                      n_outputs=2)
        gs = proj(_proj_sigmoid_kernel, h, w_gt, 3 * D, D, "proj_gate")

        q = q_proj(qkv, q_norm[l].reshape(1, Q_LORA), wq, tabq)
        k, v = kv_proj(qkv, kv_norm[l].reshape(1, KV_LORA), wkv, tabk)
        a = attention(q, k, v)

        w_pad = jnp.zeros((32, CONV_CH), F32).at[:CONV_W].set(conv_dw[l])
        cv = conv_module(z, w_pad, conv_dw_b[l].reshape(1, CONV_CH),
                         conv_ln_g[l].reshape(1, CONV_CH), conv_ln_b[l].reshape(1, CONV_CH))
        fo = fourier(fc, fs, a_lat, a_ctx)

        m = merge1(a, cv, fo, gs, w_mla_out[l].astype(BF16), w_conv_out[l].astype(BF16),
                   w_four_out[l].astype(BF16))
        wr = jnp.zeros((D, ROUTER_PAD), BF16).at[:, :N_EXP].set(w_router[l].astype(BF16))
        xt, h2, logits = merge2(m, w_out[l].astype(BF16), xt, g1, norm2[l].reshape(1, D), sh2, sc2,
                                wr)
        last = l == L - 1
        ro, ys, y_buf = moe(h2, logits, b_router[l], w_gate_up, b_gate_up, w_down, b_down, l,
                            last, y_buf)
        g2_prev = g2

    out = final_norm(xt, ro, ys, g2_prev, norm_final.reshape(1, D))
    return out.reshape(B, SEQ, D)
```

```python
import functools

import numpy as np
import jax
import jax.numpy as jnp
from jax import lax
from jax.experimental import pallas as pl
from jax.experimental.pallas import tpu as pltpu

F32 = jnp.float32
BF16 = jnp.bfloat16

D = 2048
B = 4
SEQ = 2048
CTX = 256
NB = CTX + SEQ
T = B * NB
TILE = 256
TPB = NB // TILE
NT = T // TILE
GRID_W = 64
H = 16
Q_LORA = 512
KV_LORA = 512
QK_NOPE = 128
QK_ROPE = 64
V_DIM = 128
CONV_CH = 1024
CONV_W = 31
FG = 4
FGC = 256
FCH = FG * FGC
N_EXP = 32
TOP_K = 4
D_EXP = 1024
ALPHA = 1.702
LIMIT = 7.0
EPS = 1e-6
ROPE_BASE = 10000.0
QK_SCALE = float((QK_NOPE + QK_ROPE) ** -0.5)
EXPERT_BM = 256
ROUTER_PAD = 128

ARB = "arbitrary"


def _cparams(n_axes, vmem_mb):
    return pltpu.CompilerParams(dimension_semantics=(ARB,) * n_axes,
                                vmem_limit_bytes=vmem_mb << 20)


def _mod_row(i):
    return jnp.where(i % TPB == 0, B, i // TPB)


def _ada_kernel(c_ref, w_ref, b_ref, o_ref):
    c = c_ref[...]
    s = (c * jax.nn.sigmoid(c)).astype(BF16)
    o_ref[0] = jnp.dot(s, w_ref[0].astype(BF16), preferred_element_type=F32) + b_ref[0]


def ada_mod(cc, w_ada, b_ada):
    L, _, N = w_ada.shape
    tn = 1024
    return pl.pallas_call(
        _ada_kernel,
        grid=(L, N // tn),
        in_specs=[pl.BlockSpec((8, D), lambda l, j: (0, 0)),
                  pl.BlockSpec((1, D, tn), lambda l, j: (l, 0, j)),
                  pl.BlockSpec((1, 1, tn), lambda l, j: (l, 0, j))],
        out_specs=pl.BlockSpec((1, 8, tn), lambda l, j: (l, 0, j)),
        out_shape=jax.ShapeDtypeStruct((L, 8, N), F32),
        compiler_params=_cparams(2, 40),
        name="ada_mod",
    )(cc, w_ada, b_ada.reshape(L, 1, N))


def _rms(x, w):
    ms = jnp.mean(x * x, axis=-1, keepdims=True)
    return x * lax.rsqrt(ms + EPS) * w


def _modnorm_kernel(x_ref, nw_ref, sh_ref, sc_ref, h_ref):
    y = _rms(x_ref[...], nw_ref[...])
    h_ref[...] = (y * (1.0 + sc_ref[0]) + sh_ref[0]).astype(BF16)


def _vec_spec():
    return pl.BlockSpec((1, 1, D), lambda i: (_mod_row(i), 0, 0))


def _row_spec(cols=D):
    return pl.BlockSpec((TILE, cols), lambda i: (i, 0))


def _const_spec(shape):
    return pl.BlockSpec(shape, lambda i: (0,) * len(shape))


def modnorm(x, nw, sh, sc):
    return pl.pallas_call(
        _modnorm_kernel,
        grid=(NT,),
        in_specs=[_row_spec(), _const_spec((1, D)), _vec_spec(), _vec_spec()],
        out_specs=_row_spec(),
        out_shape=jax.ShapeDtypeStruct((T, D), BF16),
        compiler_params=_cparams(1, 32),
        name="modnorm",
    )(x, nw, sh, sc)


PROJ_TM = 768


def _proj_plain_kernel(a_ref, w_ref, o_ref):
    o_ref[...] = jnp.dot(a_ref[...], w_ref[...], preferred_element_type=F32).astype(o_ref.dtype)


def _proj_sigmoid_kernel(a_ref, w_ref, o_ref):
    acc = jnp.dot(a_ref[...], w_ref[...], preferred_element_type=F32)
    o_ref[...] = jax.nn.sigmoid(acc).astype(o_ref.dtype)


def _proj_glu_kernel(a_ref, w_ref, o_ref):
    acc = jnp.dot(a_ref[...], w_ref[...], preferred_element_type=F32)
    o_ref[...] = (acc[:, :CONV_CH] * jax.nn.sigmoid(acc[:, CONV_CH:])).astype(o_ref.dtype)


def _proj_four_kernel(a_ref, w_ref, cs_ref, fc_ref, fs_ref):
    f = jnp.dot(a_ref[...], w_ref[...], preferred_element_type=F32).astype(BF16)
    cs = cs_ref[...].astype(BF16)
    for g in range(FG):
        r = jnp.dot(f[:, g * FGC:(g + 1) * FGC], cs, preferred_element_type=F32)
        fc_ref[:, g * FGC:(g + 1) * FGC] = r[:, :FGC].astype(BF16)
        fs_ref[:, g * FGC:(g + 1) * FGC] = r[:, FGC:].astype(BF16)


def proj(kernel, h, w, n_out, tn, name, extra=(), n_outputs=1, vmem_mb=48):
    K, N = w.shape
    tm = PROJ_TM
    tn_out = n_out // (N // tn)
    in_specs = [pl.BlockSpec((tm, K), lambda j, i: (i, 0)),
                pl.BlockSpec((K, tn), lambda j, i: (0, j))]
    for e in extra:
        in_specs.append(pl.BlockSpec(e.shape, lambda j, i, nd=e.ndim: (0,) * nd))
    out_spec = pl.BlockSpec((tm, tn_out), lambda j, i: (i, j))
    out_shape = jax.ShapeDtypeStruct((T, n_out), BF16)
    if n_outputs > 1:
        out_spec = [out_spec] * n_outputs
        out_shape = [out_shape] * n_outputs
    return pl.pallas_call(
        kernel,
        grid=(N // tn, T // tm),
        in_specs=in_specs,
        out_specs=out_spec,
        out_shape=out_shape,
        compiler_params=_cparams(2, vmem_mb),
        name=name,
    )(h, w, *extra)


MLA_TM = 768
MLA_HG = 4


def _lane_lt64(shape):
    return lax.broadcasted_iota(jnp.int32, shape, 1) < QK_ROPE


def _qproj_kernel(cq_ref, nw_ref, w_ref, tab_ref, q_ref, cqn_ref):
    @pl.when(pl.program_id(1) == 0)
    def _():
        cqn_ref[...] = _rms(cq_ref[...].astype(F32), nw_ref[...]).astype(BF16)

    for hh in range(MLA_HG):
        y = jnp.dot(cqn_ref[...], w_ref[hh], preferred_element_type=F32)
        a = y[:, QK_NOPE:] * tab_ref[...]
        q_ref[hh, :, :QK_NOPE] = (y[:, :QK_NOPE] * QK_SCALE).astype(BF16)
        q_ref[hh, :, QK_NOPE:] = (a + pltpu.roll(a, QK_ROPE, 1)).astype(BF16)


def q_proj(qkv, nw, wq, tabq):
    tm = MLA_TM
    return pl.pallas_call(
        _qproj_kernel,
        grid=(T // tm, H // MLA_HG),
        in_specs=[pl.BlockSpec((tm, Q_LORA), lambda i, h: (i, 0)),
                  pl.BlockSpec((1, Q_LORA), lambda i, h: (0, 0)),
                  pl.BlockSpec((MLA_HG, Q_LORA, 256), lambda i, h: (h, 0, 0)),
                  pl.BlockSpec((tm, 128), lambda i, h: (i % (NB // tm), 0))],
        out_specs=pl.BlockSpec((MLA_HG, tm, 256), lambda i, h: (h, i, 0)),
        out_shape=jax.ShapeDtypeStruct((H, T, 256), BF16),
        scratch_shapes=[pltpu.VMEM((tm, Q_LORA), BF16)],
        compiler_params=_cparams(2, 32),
        name="q_proj",
    )(qkv, nw, wq, tabq)


def _kvproj_kernel(ckv_ref, kr_ref, nw_ref, w_ref, tab_ref, k_ref, v_ref, ckvn_ref, k2_ref):
    @pl.when(pl.program_id(1) == 0)
    def _():
        ckvn_ref[...] = _rms(ckv_ref[...].astype(F32), nw_ref[...]).astype(BF16)
        a = kr_ref[...].astype(F32) * tab_ref[...]
        s = a + pltpu.roll(a, QK_ROPE, 1)
        k2_ref[...] = jnp.where(_lane_lt64(s.shape), s, 0.0).astype(BF16)

    for hh in range(MLA_HG):
        y = jnp.dot(ckvn_ref[...], w_ref[hh], preferred_element_type=F32)
        k_ref[hh, :, :QK_NOPE] = y[:, :QK_NOPE].astype(BF16)
        k_ref[hh, :, QK_NOPE:] = k2_ref[...]
        v_ref[hh] = y[:, QK_NOPE:].astype(BF16)


def kv_proj(qkv, nw, wkv, tabk):
    tm = MLA_TM
    return pl.pallas_call(
        _kvproj_kernel,
        grid=(T // tm, H // MLA_HG),
        in_specs=[pl.BlockSpec((tm, KV_LORA), lambda i, h: (i, 1)),
                  pl.BlockSpec((tm, 128), lambda i, h: (i, (Q_LORA + KV_LORA) // 128)),
                  pl.BlockSpec((1, KV_LORA), lambda i, h: (0, 0)),
                  pl.BlockSpec((MLA_HG, KV_LORA, 256), lambda i, h: (h, 0, 0)),
                  pl.BlockSpec((tm, 128), lambda i, h: (i % (NB // tm), 0))],
        out_specs=[pl.BlockSpec((MLA_HG, tm, 256), lambda i, h: (h, i, 0)),
                   pl.BlockSpec((MLA_HG, tm, V_DIM), lambda i, h: (h, i, 0))],
        out_shape=[jax.ShapeDtypeStruct((H, T, 256), BF16),
                   jax.ShapeDtypeStruct((H, T, V_DIM), BF16)],
        scratch_shapes=[pltpu.VMEM((tm, KV_LORA), BF16), pltpu.VMEM((tm, 128), BF16)],
        compiler_params=_cparams(2, 32),
        name="kv_proj",
    )(qkv, qkv, nw, wkv, tabk)


ATT_TQ = 256


def _attend(q, k, v):
    s = lax.dot_general(q, k, (((1,), (1,)), ((), ())), preferred_element_type=F32)
    m = jnp.max(s, axis=-1, keepdims=True)
    p = jnp.exp(s - m)
    l = jnp.sum(p, axis=-1, keepdims=True)
    o = jnp.dot(p.astype(BF16), v, preferred_element_type=F32)
    return (o / l).astype(BF16)


def _attn_kernel(q_ref, k_ref, v_ref, o_ref):
    o_ref[:CTX, :] = _attend(q_ref[0, :CTX, :], k_ref[0, :CTX, :], v_ref[0, :CTX, :])
    for c in range(SEQ // ATT_TQ):
        r0 = CTX + c * ATT_TQ
        o_ref[r0:r0 + ATT_TQ, :] = _attend(q_ref[0, r0:r0 + ATT_TQ, :], k_ref[0], v_ref[0])


def attention(q, k, v):
    return pl.pallas_call(
        _attn_kernel,
        grid=(B, H),
        in_specs=[pl.BlockSpec((1, NB, 256), lambda b, h: (h, b, 0)),
                  pl.BlockSpec((1, NB, 256), lambda b, h: (h, b, 0)),
                  pl.BlockSpec((1, NB, V_DIM), lambda b, h: (h, b, 0))],
        out_specs=pl.BlockSpec((NB, V_DIM), lambda b, h: (b, h)),
        out_shape=jax.ShapeDtypeStruct((T, H * V_DIM), BF16),
        compiler_params=_cparams(2, 48),
        name="attention",
    )(q, k, v)


HALO = 16
CONV_RC = 32
SUBLANES = 8
CONV_SH_ROWS = TILE + 2 * HALO - SUBLANES


def _conv_kernel(zp_ref, zc_ref, zn_ref, w_ref, b_ref, g_ref, be_ref, o_ref, buf_ref, sh_ref):
    j = pl.program_id(0) % TPB
    prev_ok = j >= 2
    next_ok = jnp.logical_and(j >= 1, j <= TPB - 2)
    buf_ref[0:HALO, :] = jnp.where(prev_ok, zp_ref[...].astype(F32), 0.0)
    buf_ref[HALO:HALO + TILE, :] = zc_ref[...].astype(F32)
    buf_ref[HALO + TILE:, :] = jnp.where(next_ok, zn_ref[...].astype(F32), 0.0)
    for s in range(1, SUBLANES):
        sh_ref[s - 1] = buf_ref[s:s + CONV_SH_ROWS, :]
    off = HALO - CONV_W // 2
    for rc in range(TILE // CONV_RC):
        r0 = rc * CONV_RC
        acc = jnp.zeros((CONV_RC, CONV_CH), F32) + b_ref[...]
        for t in range(CONV_W):
            q, s = divmod(off + t, SUBLANES)
            a0 = r0 + q * SUBLANES
            if s == 0:
                tap = buf_ref[a0:a0 + CONV_RC, :]
            else:
                tap = sh_ref[s - 1, a0:a0 + CONV_RC, :]
            acc = acc + tap * w_ref[t:t + 1, :]
        mu = jnp.mean(acc, axis=-1, keepdims=True)
        d = acc - mu
        var = jnp.mean(d * d, axis=-1, keepdims=True)
        zn = d * lax.rsqrt(var + EPS) * g_ref[...] + be_ref[...]
        o_ref[r0:r0 + CONV_RC, :] = (zn * jax.nn.sigmoid(zn)).astype(BF16)


def conv_module(z, w_pad, b, g, be):
    nh = T // HALO
    return pl.pallas_call(
        _conv_kernel,
        grid=(NT,),
        in_specs=[pl.BlockSpec((HALO, CONV_CH),
                               lambda i: (jnp.maximum(i * (TILE // HALO) - 1, 0), 0)),
                  pl.BlockSpec((TILE, CONV_CH), lambda i: (i, 0)),
                  pl.BlockSpec((HALO, CONV_CH),
                               lambda i: (jnp.minimum((i + 1) * (TILE // HALO), nh - 1), 0)),
                  _const_spec((32, CONV_CH)), _const_spec((1, CONV_CH)),
                  _const_spec((1, CONV_CH)), _const_spec((1, CONV_CH))],
        out_specs=pl.BlockSpec((TILE, CONV_CH), lambda i: (i, 0)),
        out_shape=jax.ShapeDtypeStruct((T, CONV_CH), BF16),
        scratch_shapes=[pltpu.VMEM((TILE + 2 * HALO, CONV_CH), F32),
                        pltpu.VMEM((SUBLANES - 1, CONV_SH_ROWS, CONV_CH), F32)],
        compiler_params=_cparams(1, 40),
        name="conv_module",
    )(z, z, z, w_pad, b, g, be)


def _dft_tables():
    def cs(n):
        k = np.arange(n, dtype=np.int64)
        ang = 2.0 * np.pi * ((k[:, None] * k[None, :]) % n).astype(np.float64) / n
        return np.cos(ang), np.sin(ang)

    cl, sl = cs(SEQ)
    cc, sc = cs(CTX)
    cg, sg = cs(FGC)
    a_lat = np.concatenate([cl, -sl], axis=1).astype(np.float32)
    a_ctx = np.concatenate([cc, -sc], axis=1).astype(np.float32)
    cs_ch = np.concatenate([cg, sg], axis=1).astype(np.float32)
    return jnp.asarray(a_lat), jnp.asarray(a_ctx), jnp.asarray(cs_ch)


LAT_SCALE = float((SEQ * FGC) ** -0.5)
CTX_SCALE = float((CTX * FGC) ** -0.5)


def _fourier_kernel(al_ref, ac_ref, fc_ref, fs_ref, o_ref):
    m = pl.program_id(1)

    @pl.when(m == 0)
    def _():
        r = jnp.dot(ac_ref[:, :CTX].astype(BF16), fc_ref[0, :CTX, :], preferred_element_type=F32)
        r = r + jnp.dot(ac_ref[:, CTX:].astype(BF16), fs_ref[0, :CTX, :],
                        preferred_element_type=F32)
        o_ref[...] = (r * CTX_SCALE).astype(BF16)

    @pl.when(m > 0)
    def _():
        r = jnp.dot(al_ref[:, :SEQ].astype(BF16), fc_ref[0, CTX:, :], preferred_element_type=F32)
        r = r + jnp.dot(al_ref[:, SEQ:].astype(BF16), fs_ref[0, CTX:, :],
                        preferred_element_type=F32)
        o_ref[...] = (r * LAT_SCALE).astype(BF16)


def fourier(fc, fs, a_lat, a_ctx):
    return pl.pallas_call(
        _fourier_kernel,
        grid=(B, TPB),
        in_specs=[pl.BlockSpec((TILE, 2 * SEQ), lambda b, m: (jnp.maximum(m - 1, 0), 0)),
                  pl.BlockSpec((CTX, 2 * CTX), lambda b, m: (0, 0)),
                  pl.BlockSpec((1, NB, FCH), lambda b, m: (b, 0, 0)),
                  pl.BlockSpec((1, NB, FCH), lambda b, m: (b, 0, 0))],
        out_specs=pl.BlockSpec((TILE, FCH), lambda b, m: (b * TPB + m, 0)),
        out_shape=jax.ShapeDtypeStruct((T, FCH), BF16),
        compiler_params=_cparams(2, 48),
        name="fourier",
    )(a_lat, a_ctx, fc.reshape(B, NB, FCH), fs.reshape(B, NB, FCH))


MERGE_TM = 512
MERGE_TN = 1024


def _merge1_kernel(a_ref, cv_ref, fo_ref, g0_ref, g1_ref, g2_ref, wm_ref, wc_ref, wf_ref, o_ref):
    m = g0_ref[...].astype(F32) * jnp.dot(a_ref[...], wm_ref[...], preferred_element_type=F32)
    m = m + g1_ref[...].astype(F32) * jnp.dot(cv_ref[...], wc_ref[...],
                                               preferred_element_type=F32)
    m = m + g2_ref[...].astype(F32) * jnp.dot(fo_ref[...], wf_ref[...],
                                               preferred_element_type=F32)
    o_ref[...] = m.astype(BF16)


def merge1(a, cv, fo, gs, wm, wc, wf):
    tm, tn = MERGE_TM, MERGE_TN
    nj = D // tn

    def gspec(k):
        return pl.BlockSpec((tm, tn), lambda j, i: (i, k * nj + j))

    return pl.pallas_call(
        _merge1_kernel,
        grid=(nj, T // tm),
        in_specs=[pl.BlockSpec((tm, H * V_DIM), lambda j, i: (i, 0)),
                  pl.BlockSpec((tm, CONV_CH), lambda j, i: (i, 0)),
                  pl.BlockSpec((tm, FCH), lambda j, i: (i, 0)),
                  gspec(0), gspec(1), gspec(2),
                  pl.BlockSpec((H * V_DIM, tn), lambda j, i: (0, j)),
                  pl.BlockSpec((CONV_CH, tn), lambda j, i: (0, j)),
                  pl.BlockSpec((FCH, tn), lambda j, i: (0, j))],
        out_specs=pl.BlockSpec((tm, tn), lambda j, i: (i, j)),
        out_shape=jax.ShapeDtypeStruct((T, D), BF16),
        compiler_params=_cparams(2, 48),
        name="merge1",
    )(a, cv, fo, gs, gs, gs, wm, wc, wf)


def _merge2_kernel(m_ref, w_ref, x_ref, g_ref, nw_ref, sh_ref, sc_ref, wr_ref,
                   xo_ref, h_ref, lg_ref):
    acc = jnp.dot(m_ref[...], w_ref[...], preferred_element_type=F32)
    x = x_ref[...] + g_ref[0] * acc
    xo_ref[...] = x
    h = _rms(x, nw_ref[...]) * (1.0 + sc_ref[0]) + sh_ref[0]
    h_ref[...] = h
    lg_ref[...] = jnp.dot(h.astype(BF16), wr_ref[...], preferred_element_type=F32)


def merge2(m, w_out, x, g, nw, sh, sc, wr):
    return pl.pallas_call(
        _merge2_kernel,
        grid=(NT,),
        in_specs=[_row_spec(), _const_spec((D, D)), _row_spec(), _vec_spec(),
                  _const_spec((1, D)), _vec_spec(), _vec_spec(), _const_spec((D, ROUTER_PAD))],
        out_specs=[_row_spec(), _row_spec(), _row_spec(ROUTER_PAD)],
        out_shape=[jax.ShapeDtypeStruct((T, D), F32), jax.ShapeDtypeStruct((T, D), F32),
                   jax.ShapeDtypeStruct((T, ROUTER_PAD), F32)],
        compiler_params=_cparams(1, 52),
        name="merge2",
    )(m, w_out, x, g, nw, sh, sc, wr)


W_PARTS = 8
GU_ROWS = D // W_PARTS
DN_ROWS = D_EXP // W_PARTS
N_STAGE = 2
GATHER_UNROLL = 8


def _expert_kernel(be_ref, slot_ref, nxt_ref, first_ref, c0_ref, nch_ref, nv_ref, tok_ref,
                   bgu_ref, bd_ref, h_hbm, wgu_hbm, wd_hbm, o_ref,
                   wgu_buf, wd_buf, st_gu, st_d, x_buf, sem_gu, sem_d, sem_x, *, layer):
    i = pl.program_id(0)

    def row_copy(t, r, xs):
        return pltpu.make_async_copy(h_hbm.at[pl.ds(t, 1), :], x_buf.at[xs, pl.ds(r, 1), :],
                                     sem_x.at[xs])

    def start_rows(blk, xs):
        base = blk * EXPERT_BM

        def body(r, carry):
            row_copy(tok_ref[base + r], r, xs).start()
            return carry

        lax.fori_loop(0, EXPERT_BM, body, 0, unroll=GATHER_UNROLL)

    def wait_rows(xs):
        def body(r, carry):
            row_copy(0, r, xs).wait()
            return carry

        lax.fori_loop(0, EXPERT_BM, body, 0, unroll=GATHER_UNROLL)

    def part_copies(e, c, b):
        gu = pltpu.make_async_copy(wgu_hbm.at[layer, e, pl.ds(c * GU_ROWS, GU_ROWS), :],
                                   st_gu.at[b], sem_gu.at[b])
        dn = pltpu.make_async_copy(wd_hbm.at[layer, e, pl.ds(c * DN_ROWS, DN_ROWS), :],
                                   st_d.at[b], sem_d.at[b])
        return gu, dn

    def start_part(e, c, b):
        gu, dn = part_copies(e, c, b)
        gu.start()
        dn.start()

    def take_part(e, c, dst):
        b = lax.rem(c, N_STAGE)
        gu, dn = part_copies(e, c, b)
        gu.wait()
        dn.wait()
        wgu_buf[dst, pl.ds(pl.multiple_of(c * GU_ROWS, GU_ROWS), GU_ROWS), :] = (
            st_gu[b].astype(BF16))
        wd_buf[dst, pl.ds(pl.multiple_of(c * DN_ROWS, DN_ROWS), DN_ROWS), :] = (
            st_d[b].astype(BF16))

        @pl.when(c + N_STAGE < W_PARTS)
        def _():
            start_part(e, c + N_STAGE, b)

    @pl.when(i < nv_ref[0])
    def _():
        e = be_ref[i]
        s = slot_ref[i]
        nx = nxt_ref[i]

        xs = lax.rem(i, 2)

        @pl.when(i == 0)
        def _():
            start_rows(0, 0)
            for b in range(N_STAGE):
                start_part(e, b, b)
            for c in range(W_PARTS):
                take_part(e, jnp.int32(c), s)

        @pl.when(i + 1 < nv_ref[0])
        def _():
            start_rows(i + 1, 1 - xs)

        @pl.when(jnp.logical_and(first_ref[i] == 1, nx >= 0))
        def _():
            for b in range(N_STAGE):
                start_part(nx, b, b)

        wait_rows(xs)
        x = x_buf[xs].astype(BF16)
        hgu = jnp.dot(x, wgu_buf[s], preferred_element_type=F32) + bgu_ref[0]
        glu = jnp.minimum(hgu[:, :D_EXP], LIMIT)
        lin = jnp.clip(hgu[:, D_EXP:], -LIMIT, LIMIT)
        act = glu * jax.nn.sigmoid(ALPHA * glu) * (lin + 1.0)
        y = jnp.dot(act.astype(BF16), wd_buf[s], preferred_element_type=F32) + bd_ref[0]
        o_ref[...] = y.astype(BF16)

        def body(k, carry):
            take_part(nx, c0_ref[i] + k, 1 - s)
            return carry

        lax.fori_loop(0, nch_ref[i], body, 0)

    @pl.when(i >= nv_ref[0])
    def _():
        o_ref[...] = jnp.zeros(o_ref.shape, BF16)


def _expert_plan(block_e, n_valid, nblk):
    idx = jnp.arange(nblk, dtype=jnp.int32)
    valid = idx < n_valid
    prev_e = jnp.concatenate([block_e[:1], block_e[:-1]])
    first = jnp.logical_and(valid, jnp.logical_or(idx == 0, block_e != prev_e))
    run_id = jnp.cumsum(first.astype(jnp.int32)) - 1
    n_runs = jnp.sum(first.astype(jnp.int32))
    member = jnp.logical_and(run_id[None, :] == idx[:, None], valid[None, :])
    run_len = jnp.sum(member.astype(jnp.int32), axis=1)
    run_first = jnp.min(jnp.where(member, idx[None, :], nblk), axis=1)
    run_e = block_e[jnp.minimum(run_first, nblk - 1)]
    rlen = jnp.maximum(run_len[run_id], 1)
    j = idx - run_first[run_id]
    has_next = jnp.logical_and(valid, run_id + 1 < n_runs)
    nxt = jnp.where(has_next, run_e[jnp.minimum(run_id + 1, nblk - 1)], -1)
    c_lo = (W_PARTS * j) // rlen
    c_hi = (W_PARTS * (j + 1)) // rlen
    nch = jnp.where(has_next, c_hi - c_lo, 0)
    slot = run_id % 2
    cast = lambda a: a.astype(jnp.int32)
    return cast(slot), cast(nxt), cast(first), cast(c_lo), cast(nch)


def experts(block_e, n_valid, tok_buf, hf, wgu, bgu, wd, bd, layer):
    n_rows = tok_buf.shape[0]
    nblk = n_rows // EXPERT_BM
    slot, nxt, first, c0, nch = _expert_plan(block_e, n_valid[0], nblk)

    def bmap(i, *s):
        return (layer * N_EXP + s[0][i], 0, 0)

    any_spec = pl.BlockSpec(memory_space=pl.ANY)
    grid_spec = pltpu.PrefetchScalarGridSpec(
        num_scalar_prefetch=8,
        grid=(nblk,),
        in_specs=[pl.BlockSpec((1, 1, 2 * D_EXP), bmap),
                  pl.BlockSpec((1, 1, D), bmap),
                  any_spec, any_spec, any_spec],
        out_specs=pl.BlockSpec((EXPERT_BM, D), lambda i, *s: (i, 0)),
        scratch_shapes=[pltpu.VMEM((2, D, 2 * D_EXP), BF16),
                        pltpu.VMEM((2, D_EXP, D), BF16),
                        pltpu.VMEM((N_STAGE, GU_ROWS, 2 * D_EXP), F32),
                        pltpu.VMEM((N_STAGE, DN_ROWS, D), F32),
                        pltpu.VMEM((2, EXPERT_BM, D), F32),
                        pltpu.SemaphoreType.DMA((N_STAGE,)),
                        pltpu.SemaphoreType.DMA((N_STAGE,)),
                        pltpu.SemaphoreType.DMA((2,))],
    )
    L = wgu.shape[0]
    return pl.pallas_call(
        functools.partial(_expert_kernel, layer=layer),
        grid_spec=grid_spec,
        out_shape=jax.ShapeDtypeStruct((n_rows, D), BF16),
        compiler_params=_cparams(1, 56),
        name="experts",
    )(block_e, slot, nxt, first, c0, nch, n_valid, tok_buf,
      bgu.reshape(L * N_EXP, 1, 2 * D_EXP), bd.reshape(L * N_EXP, 1, D), hf, wgu, wd)


NEG_BIG = -1e30
RO_E, RO_RANK, RO_GATE = 0, TOP_K, 2 * TOP_K


def _router_kernel(lg_ref, b_ref, tri_ref, ro_ref, cnt_ref, base_ref):
    @pl.when(pl.program_id(0) == 0)
    def _():
        base_ref[...] = jnp.zeros(base_ref.shape, F32)

    lane = lax.broadcasted_iota(jnp.int32, (TILE, ROUTER_PAD), 1)
    lanef = lane.astype(F32)
    lg = jnp.where(lane < N_EXP, lg_ref[...] + b_ref[...], NEG_BIG)
    vals, idxs, hots = [], [], []
    for _ in range(TOP_K):
        m = jnp.max(lg, axis=-1, keepdims=True)
        idx = jnp.min(jnp.where(lg == m, lanef, float(ROUTER_PAD)), axis=-1, keepdims=True)
        hot = lanef == idx
        lg = jnp.where(hot, NEG_BIG, lg)
        vals.append(m)
        idxs.append(idx)
        hots.append(hot)
    ex = [jnp.exp(v - vals[0]) for v in vals]
    den = ex[0] + ex[1] + ex[2] + ex[3]
    chosen = jnp.zeros((TILE, ROUTER_PAD), F32)
    for hot in hots:
        chosen = jnp.where(hot, 1.0, chosen)
    before = jnp.dot(tri_ref[...], chosen.astype(BF16), preferred_element_type=F32)
    tot = before + base_ref[0:1, :]
    rec = jnp.zeros((TILE, ROUTER_PAD), F32)
    for k in range(TOP_K):
        rank = jnp.sum(jnp.where(hots[k], tot, 0.0), axis=-1, keepdims=True)
        rec = jnp.where(lane == RO_E + k, idxs[k], rec)
        rec = jnp.where(lane == RO_RANK + k, rank, rec)
        rec = jnp.where(lane == RO_GATE + k, ex[k] / den, rec)
    ro_ref[...] = rec
    base_ref[...] = base_ref[...] + jnp.sum(chosen, axis=0, keepdims=True)
    cnt_ref[...] = base_ref[...]


def router(logits, b_pad, tri, latent_only):
    if latent_only:
        ntl = SEQ // TILE
        n_tiles = B * ntl
        in_map = lambda i: ((i // ntl) * TPB + 1 + i % ntl, 0)
    else:
        n_tiles = NT
        in_map = lambda i: (i, 0)
    return pl.pallas_call(
        _router_kernel,
        grid=(n_tiles,),
        in_specs=[pl.BlockSpec((TILE, ROUTER_PAD), in_map),
                  _const_spec((1, ROUTER_PAD)), _const_spec((TILE, TILE))],
        out_specs=[_row_spec(ROUTER_PAD), _const_spec((8, ROUTER_PAD))],
        out_shape=[jax.ShapeDtypeStruct((n_tiles * TILE, ROUTER_PAD), F32),
                   jax.ShapeDtypeStruct((8, ROUTER_PAD), F32)],
        scratch_shapes=[pltpu.VMEM((8, ROUTER_PAD), F32)],
        compiler_params=_cparams(1, 32),
        name="router",
    )(logits, b_pad, tri)


def moe(hf, logits, b_router, wgu, bgu, wd, bd, layer, latent_only):
    b_pad = jnp.zeros((1, ROUTER_PAD), F32).at[0, :N_EXP].set(b_router)
    tri = jnp.asarray(np.tril(np.ones((TILE, TILE), np.float32), -1), dtype=BF16)
    ro, cnt = router(logits, b_pad, tri, latent_only)
    n_tok = ro.shape[0]
    if latent_only:
        tok_ids = (jnp.arange(B, dtype=jnp.int32)[:, None] * NB + CTX
                   + jnp.arange(SEQ, dtype=jnp.int32)[None, :]).reshape(-1)
    else:
        tok_ids = jnp.arange(T, dtype=jnp.int32)
    e = ro[:, RO_E:RO_E + TOP_K].astype(jnp.int32)
    rank = ro[:, RO_RANK:RO_RANK + TOP_K].astype(jnp.int32)
    counts = cnt[0, :N_EXP].astype(jnp.int32)
    padded = (counts + EXPERT_BM - 1) // EXPERT_BM * EXPERT_BM
    pends = jnp.cumsum(padded)
    pstarts = pends - padded
    dest = pstarts[e] + rank
    n_pair = n_tok * TOP_K
    nblk = n_pair // EXPERT_BM + N_EXP
    n_rows = nblk * EXPERT_BM
    tok_buf = jnp.zeros((n_rows,), jnp.int32).at[dest.reshape(-1)].set(jnp.repeat(tok_ids, TOP_K))
    n_valid = (pends[-1] // EXPERT_BM).astype(jnp.int32)
    blk = jnp.arange(nblk, dtype=jnp.int32)
    block_e = jnp.minimum(jnp.sum((pends[None, :] <= blk[:, None] * EXPERT_BM).astype(jnp.int32),
                                  axis=1), N_EXP - 1)
    block_e = jnp.where(blk < n_valid, block_e, block_e[n_valid - 1])
    yb = experts(block_e.astype(jnp.int32), n_valid.reshape(1), tok_buf, hf, wgu, bgu, wd, bd,
                 layer)
    ys = [yb[dest[:, k]] for k in range(TOP_K)]
    return ro, ys


def _combine(ro_ref, y_refs):
    ro = ro_ref[...]
    acc = ro[:, RO_GATE:RO_GATE + 1] * y_refs[0][...].astype(F32)
    for k in range(1, TOP_K):
        acc = acc + ro[:, RO_GATE + k:RO_GATE + k + 1] * y_refs[k][...].astype(F32)
    return acc


def _modnorm_moe_kernel(x_ref, ro_ref, y0_ref, y1_ref, y2_ref, y3_ref, g_ref, nw_ref, sh_ref,
                        sc_ref, xo_ref, h_ref):
    x = x_ref[...] + g_ref[0] * _combine(ro_ref, (y0_ref, y1_ref, y2_ref, y3_ref))
    xo_ref[...] = x
    y = _rms(x, nw_ref[...])
    h_ref[...] = (y * (1.0 + sc_ref[0]) + sh_ref[0]).astype(BF16)


def modnorm_moe(x, ro, ys, g, nw, sh, sc):
    return pl.pallas_call(
        _modnorm_moe_kernel,
        grid=(NT,),
        in_specs=[_row_spec(), _row_spec(ROUTER_PAD)] + [_row_spec()] * TOP_K
                 + [_vec_spec(), _const_spec((1, D)), _vec_spec(), _vec_spec()],
        out_specs=[_row_spec(), _row_spec()],
        out_shape=[jax.ShapeDtypeStruct((T, D), F32), jax.ShapeDtypeStruct((T, D), BF16)],
        compiler_params=_cparams(1, 40),
        name="modnorm_moe",
    )(x, ro, *ys, g, nw, sh, sc)


def _final_kernel(x_ref, ro_ref, y0_ref, y1_ref, y2_ref, y3_ref, g_ref, nw_ref, o_ref):
    x = x_ref[...] + g_ref[0] * _combine(ro_ref, (y0_ref, y1_ref, y2_ref, y3_ref))
    o_ref[...] = _rms(x, nw_ref[...])


def final_norm(x, ro, ys, g, nw):
    ntl = SEQ // TILE
    lat = lambda cols: pl.BlockSpec((TILE, cols), lambda b, m: (b * ntl + m, 0))
    return pl.pallas_call(
        _final_kernel,
        grid=(B, ntl),
        in_specs=[pl.BlockSpec((TILE, D), lambda b, m: (b * TPB + 1 + m, 0)),
                  lat(ROUTER_PAD)] + [lat(D)] * TOP_K
                 + [pl.BlockSpec((1, 1, D), lambda b, m: (b, 0, 0)),
                    pl.BlockSpec((1, D), lambda b, m: (0, 0))],
        out_specs=lat(D),
        out_shape=jax.ShapeDtypeStruct((B * SEQ, D), F32),
        compiler_params=_cparams(2, 32),
        name="final_norm",
    )(x, ro, *ys, g, nw)


def _rope_tables():
    t = np.arange(SEQ)
    row = (t // GRID_W).astype(np.float64)
    col = (t % GRID_W).astype(np.float64)
    half = QK_ROPE // 2
    inv = ROPE_BASE ** (-np.arange(0, half, 2, dtype=np.float64) / half)
    ang = np.concatenate([row[:, None] * inv, col[:, None] * inv], axis=-1)
    cos, sin = np.cos(ang), np.sin(ang)
    lat = np.concatenate([cos, cos, -sin, sin], axis=-1)
    ctx = np.concatenate([np.ones((CTX, QK_ROPE)), np.zeros((CTX, QK_ROPE))], axis=-1)
    tab = np.concatenate([ctx, lat], axis=0).astype(np.float32)
    return jnp.asarray(tab), jnp.asarray(tab * np.float32(QK_SCALE))


def _swap_halves(w):
    half = QK_ROPE // 2
    return jnp.concatenate([w[..., half:], w[..., :half]], axis=-1)


IN_O1 = Q_LORA + KV_LORA
IN_O2 = IN_O1 + QK_ROPE
IN_O3 = IN_O2 + 2 * CONV_CH
IN_O4 = IN_O3 + FCH
IN_COLS = IN_O4 + 3 * D
QKV_COLS = IN_O1 + 2 * QK_ROPE
PREP_TR = 256


def _prep_kernel(w_ref, qkv_ref, u_ref, f_ref, gt_ref):
    qkv_ref[:, :IN_O1] = w_ref[0, :, :IN_O1].astype(BF16)
    kb = w_ref[0, :, IN_O1:IN_O1 + 128]
    lane = lax.broadcasted_iota(jnp.int32, kb.shape, 1)
    half = QK_ROPE // 2
    swapped = jnp.where(lane < QK_ROPE, kb,
                        jnp.where(lane < QK_ROPE + half, pltpu.roll(kb, half, 1),
                                  pltpu.roll(kb, QK_ROPE + half, 1)))
    qkv_ref[:, IN_O1:] = swapped.astype(BF16)
    u_ref[...] = w_ref[0, :, IN_O2:IN_O3].astype(BF16)
    f_ref[...] = w_ref[0, :, IN_O3:IN_O4].astype(BF16)
    gt_ref[...] = w_ref[0, :, IN_O4:].astype(BF16)


def prep_w_in(w_in, l):
    widths = (QKV_COLS, 2 * CONV_CH, FCH, 3 * D)
    return pl.pallas_call(
        _prep_kernel,
        grid=(D // PREP_TR,),
        in_specs=[pl.BlockSpec((1, PREP_TR, IN_COLS), lambda i: (l, i, 0))],
        out_specs=[pl.BlockSpec((PREP_TR, n), lambda i: (i, 0)) for n in widths],
        out_shape=[jax.ShapeDtypeStruct((D, n), BF16) for n in widths],
        compiler_params=_cparams(1, 48),
        name="prep_w_in",
    )(w_in)


def _layer_weights(w_uq, w_ukv):
    wq = w_uq.reshape(Q_LORA, H, QK_NOPE + QK_ROPE)
    wq_r = wq[:, :, QK_NOPE:]
    wq = jnp.concatenate([wq[:, :, :QK_NOPE], wq_r, _swap_halves(wq_r)], axis=-1)
    wq = jnp.transpose(wq, (1, 0, 2)).astype(BF16)
    wkv = jnp.transpose(w_ukv.reshape(KV_LORA, H, QK_NOPE + V_DIM), (1, 0, 2)).astype(BF16)
    return wq, wkv


def kernel(x, c, ctx, c_ctx, w_ada, b_ada, norm1, w_in, q_norm, kv_norm, w_uq, w_ukv, w_mla_out,
           conv_dw, conv_dw_b, conv_ln_g, conv_ln_b, w_conv_out, w_four_out, w_out, norm2,
           w_router, b_router, w_gate_up, b_gate_up, w_down, b_down, norm_final):
    L = w_ada.shape[0]
    xt = jnp.concatenate([ctx, x], axis=1).reshape(T, D)
    cc = jnp.zeros((8, D), F32).at[:B].set(c).at[B].set(c_ctx)
    mod = ada_mod(cc, w_ada, b_ada).reshape(L, 8, 6, 1, D)
    tabk, tabq = _rope_tables()
    a_lat, a_ctx, cs_ch = _dft_tables()

    ro = ys = None
    g2_prev = None
    for l in range(L):
        sh1, sc1, g1, sh2, sc2, g2 = [mod[l, :, k] for k in range(6)]
        w_qkv, w_u, w_f, w_gt = prep_w_in(w_in, l)
        wq, wkv = _layer_weights(w_uq[l], w_ukv[l])
        nw1 = norm1[l].reshape(1, D)
        if l == 0:
            h = modnorm(xt, nw1, sh1, sc1)
        else:
            xt, h = modnorm_moe(xt, ro, ys, g2_prev, nw1, sh1, sc1)

        qkv = proj(_proj_plain_kernel, h, w_qkv, QKV_COLS, QKV_COLS, "proj_qkv")
        z = proj(_proj_glu_kernel, h, w_u, CONV_CH, 2 * CONV_CH, "proj_glu")
        fc, fs = proj(_proj_four_kernel, h, w_f, FCH, FCH, "proj_four", extra=(cs_ch,),
                      n_outputs=2)
        gs = proj(_proj_sigmoid_kernel, h, w_gt, 3 * D, D, "proj_gate")

        q = q_proj(qkv, q_norm[l].reshape(1, Q_LORA), wq, tabq)
        k, v = kv_proj(qkv, kv_norm[l].reshape(1, KV_LORA), wkv, tabk)
        a = attention(q, k, v)

        w_pad = jnp.zeros((32, CONV_CH), F32).at[:CONV_W].set(conv_dw[l])
        cv = conv_module(z, w_pad, conv_dw_b[l].reshape(1, CONV_CH),
                         conv_ln_g[l].reshape(1, CONV_CH), conv_ln_b[l].reshape(1, CONV_CH))
        fo = fourier(fc, fs, a_lat, a_ctx)

        m = merge1(a, cv, fo, gs, w_mla_out[l].astype(BF16), w_conv_out[l].astype(BF16),
                   w_four_out[l].astype(BF16))
        wr = jnp.zeros((D, ROUTER_PAD), BF16).at[:, :N_EXP].set(w_router[l].astype(BF16))
        xt, h2, logits = merge2(m, w_out[l].astype(BF16), xt, g1, norm2[l].reshape(1, D), sh2, sc2,
                                wr)
        last = l == L - 1
        ro, ys = moe(h2, logits, b_router[l], w_gate_up, b_gate_up, w_down, b_down, l, last)
        g2_prev = g2

    out = final_norm(xt, ro, ys, g2_prev, norm_final.reshape(1, D))
    return out.reshape(B, SEQ, D)
```

```python
import functools

import numpy as np
import jax
import jax.numpy as jnp
from jax import lax
from jax.experimental import pallas as pl
from jax.experimental.pallas import tpu as pltpu

F32 = jnp.float32
BF16 = jnp.bfloat16

D = 2048
B = 4
SEQ = 2048
CTX = 256
NB = CTX + SEQ
T = B * NB
TILE = 256
TPB = NB // TILE
NT = T // TILE
GRID_W = 64
H = 16
Q_LORA = 512
KV_LORA = 512
QK_NOPE = 128
QK_ROPE = 64
V_DIM = 128
CONV_CH = 1024
CONV_W = 31
FG = 4
FGC = 256
FCH = FG * FGC
N_EXP = 32
TOP_K = 4
D_EXP = 1024
ALPHA = 1.702
LIMIT = 7.0
EPS = 1e-6
ROPE_BASE = 10000.0
QK_SCALE = float((QK_NOPE + QK_ROPE) ** -0.5)
EXPERT_BM = 256
ROUTER_PAD = 128

ARB = "arbitrary"


def _cparams(n_axes, vmem_mb):
    return pltpu.CompilerParams(dimension_semantics=(ARB,) * n_axes,
                                vmem_limit_bytes=vmem_mb << 20)


def _mod_row(i):
    return jnp.where(i % TPB == 0, B, i // TPB)


def _ada_kernel(c_ref, w_ref, b_ref, o_ref):
    c = c_ref[...]
    s = (c * jax.nn.sigmoid(c)).astype(BF16)
    o_ref[0] = jnp.dot(s, w_ref[0].astype(BF16), preferred_element_type=F32) + b_ref[0]


def ada_mod(cc, w_ada, b_ada):
    L, _, N = w_ada.shape
    tn = 1024
    return pl.pallas_call(
        _ada_kernel,
        grid=(L, N // tn),
        in_specs=[pl.BlockSpec((8, D), lambda l, j: (0, 0)),
                  pl.BlockSpec((1, D, tn), lambda l, j: (l, 0, j)),
                  pl.BlockSpec((1, 1, tn), lambda l, j: (l, 0, j))],
        out_specs=pl.BlockSpec((1, 8, tn), lambda l, j: (l, 0, j)),
        out_shape=jax.ShapeDtypeStruct((L, 8, N), F32),
        compiler_params=_cparams(2, 40),
        name="ada_mod",
    )(cc, w_ada, b_ada.reshape(L, 1, N))


def _rms(x, w):
    ms = jnp.mean(x * x, axis=-1, keepdims=True)
    return x * lax.rsqrt(ms + EPS) * w


def _modnorm_kernel(x_ref, nw_ref, sh_ref, sc_ref, h_ref):
    y = _rms(x_ref[...], nw_ref[...])
    h_ref[...] = (y * (1.0 + sc_ref[0]) + sh_ref[0]).astype(BF16)


def _vec_spec():
    return pl.BlockSpec((1, 1, D), lambda i: (_mod_row(i), 0, 0))


def _row_spec(cols=D):
    return pl.BlockSpec((TILE, cols), lambda i: (i, 0))


def _const_spec(shape):
    return pl.BlockSpec(shape, lambda i: (0,) * len(shape))


def modnorm(x, nw, sh, sc):
    return pl.pallas_call(
        _modnorm_kernel,
        grid=(NT,),
        in_specs=[_row_spec(), _const_spec((1, D)), _vec_spec(), _vec_spec()],
        out_specs=_row_spec(),
        out_shape=jax.ShapeDtypeStruct((T, D), BF16),
        compiler_params=_cparams(1, 32),
        name="modnorm",
    )(x, nw, sh, sc)


PROJ_TM = 768


def _proj_plain_kernel(a_ref, w_ref, o_ref):
    o_ref[...] = jnp.dot(a_ref[...], w_ref[...], preferred_element_type=F32).astype(o_ref.dtype)


def _proj_sigmoid_kernel(a_ref, w_ref, o_ref):
    acc = jnp.dot(a_ref[...], w_ref[...], preferred_element_type=F32)
    o_ref[...] = jax.nn.sigmoid(acc).astype(o_ref.dtype)


def _proj_glu_kernel(a_ref, w_ref, o_ref):
    acc = jnp.dot(a_ref[...], w_ref[...], preferred_element_type=F32)
    o_ref[...] = (acc[:, :CONV_CH] * jax.nn.sigmoid(acc[:, CONV_CH:])).astype(o_ref.dtype)


def _proj_four_kernel(a_ref, w_ref, cs_ref, fc_ref, fs_ref):
    f = jnp.dot(a_ref[...], w_ref[...], preferred_element_type=F32).astype(BF16)
    cs = cs_ref[...].astype(BF16)
    for g in range(FG):
        r = jnp.dot(f[:, g * FGC:(g + 1) * FGC], cs, preferred_element_type=F32)
        fc_ref[:, g * FGC:(g + 1) * FGC] = r[:, :FGC].astype(BF16)
        fs_ref[:, g * FGC:(g + 1) * FGC] = r[:, FGC:].astype(BF16)


def proj(kernel, h, w, n_out, tn, name, extra=(), n_outputs=1, vmem_mb=48):
    K, N = w.shape
    tm = PROJ_TM
    tn_out = n_out // (N // tn)
    in_specs = [pl.BlockSpec((tm, K), lambda j, i: (i, 0)),
                pl.BlockSpec((K, tn), lambda j, i: (0, j))]
    for e in extra:
        in_specs.append(pl.BlockSpec(e.shape, lambda j, i, nd=e.ndim: (0,) * nd))
    out_spec = pl.BlockSpec((tm, tn_out), lambda j, i: (i, j))
    out_shape = jax.ShapeDtypeStruct((T, n_out), BF16)
    if n_outputs > 1:
        out_spec = [out_spec] * n_outputs
        out_shape = [out_shape] * n_outputs
    return pl.pallas_call(
        kernel,
        grid=(N // tn, T // tm),
        in_specs=in_specs,
        out_specs=out_spec,
        out_shape=out_shape,
        compiler_params=_cparams(2, vmem_mb),
        name=name,
    )(h, w, *extra)


MLA_TM = 768
MLA_HG = 4


def _lane_lt64(shape):
    return lax.broadcasted_iota(jnp.int32, shape, 1) < QK_ROPE


def _qproj_kernel(cq_ref, nw_ref, w_ref, tab_ref, q_ref, cqn_ref):
    @pl.when(pl.program_id(1) == 0)
    def _():
        cqn_ref[...] = _rms(cq_ref[...].astype(F32), nw_ref[...]).astype(BF16)

    for hh in range(MLA_HG):
        y = jnp.dot(cqn_ref[...], w_ref[hh], preferred_element_type=F32)
        a = y[:, QK_NOPE:] * tab_ref[...]
        q_ref[hh, :, :QK_NOPE] = (y[:, :QK_NOPE] * QK_SCALE).astype(BF16)
        q_ref[hh, :, QK_NOPE:] = (a + pltpu.roll(a, QK_ROPE, 1)).astype(BF16)


def q_proj(qkv, nw, wq, tabq):
    tm = MLA_TM
    return pl.pallas_call(
        _qproj_kernel,
        grid=(T // tm, H // MLA_HG),
        in_specs=[pl.BlockSpec((tm, Q_LORA), lambda i, h: (i, 0)),
                  pl.BlockSpec((1, Q_LORA), lambda i, h: (0, 0)),
                  pl.BlockSpec((MLA_HG, Q_LORA, 256), lambda i, h: (h, 0, 0)),
                  pl.BlockSpec((tm, 128), lambda i, h: (i % (NB // tm), 0))],
        out_specs=pl.BlockSpec((MLA_HG, tm, 256), lambda i, h: (h, i, 0)),
        out_shape=jax.ShapeDtypeStruct((H, T, 256), BF16),
        scratch_shapes=[pltpu.VMEM((tm, Q_LORA), BF16)],
        compiler_params=_cparams(2, 32),
        name="q_proj",
    )(qkv, nw, wq, tabq)


def _kvproj_kernel(ckv_ref, kr_ref, nw_ref, w_ref, tab_ref, k_ref, v_ref, ckvn_ref, k2_ref):
    @pl.when(pl.program_id(1) == 0)
    def _():
        ckvn_ref[...] = _rms(ckv_ref[...].astype(F32), nw_ref[...]).astype(BF16)
        a = kr_ref[...].astype(F32) * tab_ref[...]
        s = a + pltpu.roll(a, QK_ROPE, 1)
        k2_ref[...] = jnp.where(_lane_lt64(s.shape), s, 0.0).astype(BF16)

    for hh in range(MLA_HG):
        y = jnp.dot(ckvn_ref[...], w_ref[hh], preferred_element_type=F32)
        k_ref[hh, :, :QK_NOPE] = y[:, :QK_NOPE].astype(BF16)
        k_ref[hh, :, QK_NOPE:] = k2_ref[...]
        v_ref[hh] = y[:, QK_NOPE:].astype(BF16)


def kv_proj(qkv, nw, wkv, tabk):
    tm = MLA_TM
    return pl.pallas_call(
        _kvproj_kernel,
        grid=(T // tm, H // MLA_HG),
        in_specs=[pl.BlockSpec((tm, KV_LORA), lambda i, h: (i, 1)),
                  pl.BlockSpec((tm, 128), lambda i, h: (i, (Q_LORA + KV_LORA) // 128)),
                  pl.BlockSpec((1, KV_LORA), lambda i, h: (0, 0)),
                  pl.BlockSpec((MLA_HG, KV_LORA, 256), lambda i, h: (h, 0, 0)),
                  pl.BlockSpec((tm, 128), lambda i, h: (i % (NB // tm), 0))],
        out_specs=[pl.BlockSpec((MLA_HG, tm, 256), lambda i, h: (h, i, 0)),
                   pl.BlockSpec((MLA_HG, tm, V_DIM), lambda i, h: (h, i, 0))],
        out_shape=[jax.ShapeDtypeStruct((H, T, 256), BF16),
                   jax.ShapeDtypeStruct((H, T, V_DIM), BF16)],
        scratch_shapes=[pltpu.VMEM((tm, KV_LORA), BF16), pltpu.VMEM((tm, 128), BF16)],
        compiler_params=_cparams(2, 32),
        name="kv_proj",
    )(qkv, qkv, nw, wkv, tabk)


ATT_TQ = 256


def _attend(q, k, v):
    s = lax.dot_general(q, k, (((1,), (1,)), ((), ())), preferred_element_type=F32)
    m = jnp.max(s, axis=-1, keepdims=True)
    p = jnp.exp(s - m)
    l = jnp.sum(p, axis=-1, keepdims=True)
    o = jnp.dot(p.astype(BF16), v, preferred_element_type=F32)
    return (o / l).astype(BF16)


def _attn_kernel(q_ref, k_ref, v_ref, o_ref):
    o_ref[:CTX, :] = _attend(q_ref[0, :CTX, :], k_ref[0, :CTX, :], v_ref[0, :CTX, :])
    for c in range(SEQ // ATT_TQ):
        r0 = CTX + c * ATT_TQ
        o_ref[r0:r0 + ATT_TQ, :] = _attend(q_ref[0, r0:r0 + ATT_TQ, :], k_ref[0], v_ref[0])


def attention(q, k, v):
    return pl.pallas_call(
        _attn_kernel,
        grid=(B, H),
        in_specs=[pl.BlockSpec((1, NB, 256), lambda b, h: (h, b, 0)),
                  pl.BlockSpec((1, NB, 256), lambda b, h: (h, b, 0)),
                  pl.BlockSpec((1, NB, V_DIM), lambda b, h: (h, b, 0))],
        out_specs=pl.BlockSpec((NB, V_DIM), lambda b, h: (b, h)),
        out_shape=jax.ShapeDtypeStruct((T, H * V_DIM), BF16),
        compiler_params=_cparams(2, 48),
        name="attention",
    )(q, k, v)


HALO = 16
CONV_RC = 32
SUBLANES = 8
CONV_SH_ROWS = TILE + 2 * HALO - SUBLANES


def _conv_kernel(zp_ref, zc_ref, zn_ref, w_ref, b_ref, g_ref, be_ref, o_ref, buf_ref, sh_ref):
    j = pl.program_id(0) % TPB
    prev_ok = j >= 2
    next_ok = jnp.logical_and(j >= 1, j <= TPB - 2)
    buf_ref[0:HALO, :] = jnp.where(prev_ok, zp_ref[...].astype(F32), 0.0)
    buf_ref[HALO:HALO + TILE, :] = zc_ref[...].astype(F32)
    buf_ref[HALO + TILE:, :] = jnp.where(next_ok, zn_ref[...].astype(F32), 0.0)
    for s in range(1, SUBLANES):
        sh_ref[s - 1] = buf_ref[s:s + CONV_SH_ROWS, :]
    off = HALO - CONV_W // 2
    for rc in range(TILE // CONV_RC):
        r0 = rc * CONV_RC
        acc = jnp.zeros((CONV_RC, CONV_CH), F32) + b_ref[...]
        for t in range(CONV_W):
            q, s = divmod(off + t, SUBLANES)
            a0 = r0 + q * SUBLANES
            if s == 0:
                tap = buf_ref[a0:a0 + CONV_RC, :]
            else:
                tap = sh_ref[s - 1, a0:a0 + CONV_RC, :]
            acc = acc + tap * w_ref[t:t + 1, :]
        mu = jnp.mean(acc, axis=-1, keepdims=True)
        d = acc - mu
        var = jnp.mean(d * d, axis=-1, keepdims=True)
        zn = d * lax.rsqrt(var + EPS) * g_ref[...] + be_ref[...]
        o_ref[r0:r0 + CONV_RC, :] = (zn * jax.nn.sigmoid(zn)).astype(BF16)


def conv_module(z, w_pad, b, g, be):
    nh = T // HALO
    return pl.pallas_call(
        _conv_kernel,
        grid=(NT,),
        in_specs=[pl.BlockSpec((HALO, CONV_CH),
                               lambda i: (jnp.maximum(i * (TILE // HALO) - 1, 0), 0)),
                  pl.BlockSpec((TILE, CONV_CH), lambda i: (i, 0)),
                  pl.BlockSpec((HALO, CONV_CH),
                               lambda i: (jnp.minimum((i + 1) * (TILE // HALO), nh - 1), 0)),
                  _const_spec((32, CONV_CH)), _const_spec((1, CONV_CH)),
                  _const_spec((1, CONV_CH)), _const_spec((1, CONV_CH))],
        out_specs=pl.BlockSpec((TILE, CONV_CH), lambda i: (i, 0)),
        out_shape=jax.ShapeDtypeStruct((T, CONV_CH), BF16),
        scratch_shapes=[pltpu.VMEM((TILE + 2 * HALO, CONV_CH), F32),
                        pltpu.VMEM((SUBLANES - 1, CONV_SH_ROWS, CONV_CH), F32)],
        compiler_params=_cparams(1, 40),
        name="conv_module",
    )(z, z, z, w_pad, b, g, be)


def _dft_tables():
    def cs(n):
        k = np.arange(n, dtype=np.int64)
        ang = 2.0 * np.pi * ((k[:, None] * k[None, :]) % n).astype(np.float64) / n
        return np.cos(ang), np.sin(ang)

    cl, sl = cs(SEQ)
    cc, sc = cs(CTX)
    cg, sg = cs(FGC)
    a_lat = np.concatenate([cl, -sl], axis=1).astype(np.float32)
    a_ctx = np.concatenate([cc, -sc], axis=1).astype(np.float32)
    cs_ch = np.concatenate([cg, sg], axis=1).astype(np.float32)
    return jnp.asarray(a_lat), jnp.asarray(a_ctx), jnp.asarray(cs_ch)


LAT_SCALE = float((SEQ * FGC) ** -0.5)
CTX_SCALE = float((CTX * FGC) ** -0.5)


def _fourier_kernel(al_ref, ac_ref, fc_ref, fs_ref, o_ref):
    m = pl.program_id(1)

    @pl.when(m == 0)
    def _():
        r = jnp.dot(ac_ref[:, :CTX].astype(BF16), fc_ref[0, :CTX, :], preferred_element_type=F32)
        r = r + jnp.dot(ac_ref[:, CTX:].astype(BF16), fs_ref[0, :CTX, :],
                        preferred_element_type=F32)
        o_ref[...] = (r * CTX_SCALE).astype(BF16)

    @pl.when(m > 0)
    def _():
        r = jnp.dot(al_ref[:, :SEQ].astype(BF16), fc_ref[0, CTX:, :], preferred_element_type=F32)
        r = r + jnp.dot(al_ref[:, SEQ:].astype(BF16), fs_ref[0, CTX:, :],
                        preferred_element_type=F32)
        o_ref[...] = (r * LAT_SCALE).astype(BF16)


def fourier(fc, fs, a_lat, a_ctx):
    return pl.pallas_call(
        _fourier_kernel,
        grid=(B, TPB),
        in_specs=[pl.BlockSpec((TILE, 2 * SEQ), lambda b, m: (jnp.maximum(m - 1, 0), 0)),
                  pl.BlockSpec((CTX, 2 * CTX), lambda b, m: (0, 0)),
                  pl.BlockSpec((1, NB, FCH), lambda b, m: (b, 0, 0)),
                  pl.BlockSpec((1, NB, FCH), lambda b, m: (b, 0, 0))],
        out_specs=pl.BlockSpec((TILE, FCH), lambda b, m: (b * TPB + m, 0)),
        out_shape=jax.ShapeDtypeStruct((T, FCH), BF16),
        compiler_params=_cparams(2, 48),
        name="fourier",
    )(a_lat, a_ctx, fc.reshape(B, NB, FCH), fs.reshape(B, NB, FCH))


MERGE_TM = 512
MERGE_TN = 1024


def _merge1_kernel(a_ref, cv_ref, fo_ref, g0_ref, g1_ref, g2_ref, wm_ref, wc_ref, wf_ref, o_ref):
    m = g0_ref[...].astype(F32) * jnp.dot(a_ref[...], wm_ref[...], preferred_element_type=F32)
    m = m + g1_ref[...].astype(F32) * jnp.dot(cv_ref[...], wc_ref[...],
                                               preferred_element_type=F32)
    m = m + g2_ref[...].astype(F32) * jnp.dot(fo_ref[...], wf_ref[...],
                                               preferred_element_type=F32)
    o_ref[...] = m.astype(BF16)


def merge1(a, cv, fo, gs, wm, wc, wf):
    tm, tn = MERGE_TM, MERGE_TN
    nj = D // tn

    def gspec(k):
        return pl.BlockSpec((tm, tn), lambda j, i: (i, k * nj + j))

    return pl.pallas_call(
        _merge1_kernel,
        grid=(nj, T // tm),
        in_specs=[pl.BlockSpec((tm, H * V_DIM), lambda j, i: (i, 0)),
                  pl.BlockSpec((tm, CONV_CH), lambda j, i: (i, 0)),
                  pl.BlockSpec((tm, FCH), lambda j, i: (i, 0)),
                  gspec(0), gspec(1), gspec(2),
                  pl.BlockSpec((H * V_DIM, tn), lambda j, i: (0, j)),
                  pl.BlockSpec((CONV_CH, tn), lambda j, i: (0, j)),
                  pl.BlockSpec((FCH, tn), lambda j, i: (0, j))],
        out_specs=pl.BlockSpec((tm, tn), lambda j, i: (i, j)),
        out_shape=jax.ShapeDtypeStruct((T, D), BF16),
        compiler_params=_cparams(2, 48),
        name="merge1",
    )(a, cv, fo, gs, gs, gs, wm, wc, wf)


def _merge2_kernel(m_ref, w_ref, x_ref, g_ref, nw_ref, sh_ref, sc_ref, wr_ref,
                   xo_ref, h_ref, lg_ref):
    acc = jnp.dot(m_ref[...], w_ref[...], preferred_element_type=F32)
    x = x_ref[...] + g_ref[0] * acc
    xo_ref[...] = x
    h = _rms(x, nw_ref[...]) * (1.0 + sc_ref[0]) + sh_ref[0]
    h_ref[...] = h
    lg_ref[...] = jnp.dot(h.astype(BF16), wr_ref[...], preferred_element_type=F32)


def merge2(m, w_out, x, g, nw, sh, sc, wr):
    return pl.pallas_call(
        _merge2_kernel,
        grid=(NT,),
        in_specs=[_row_spec(), _const_spec((D, D)), _row_spec(), _vec_spec(),
                  _const_spec((1, D)), _vec_spec(), _vec_spec(), _const_spec((D, ROUTER_PAD))],
        out_specs=[_row_spec(), _row_spec(), _row_spec(ROUTER_PAD)],
        out_shape=[jax.ShapeDtypeStruct((T, D), F32), jax.ShapeDtypeStruct((T, D), F32),
                   jax.ShapeDtypeStruct((T, ROUTER_PAD), F32)],
        compiler_params=_cparams(1, 52),
        name="merge2",
    )(m, w_out, x, g, nw, sh, sc, wr)


W_PARTS = 8
GU_ROWS = D // W_PARTS
DN_ROWS = D_EXP // W_PARTS
N_STAGE = 2
GATHER_UNROLL = 8
X_SLOTS = 3


def _expert_kernel(be_ref, slot_ref, nxt_ref, first_ref, c0_ref, nch_ref, nv_ref, tok_ref,
                   bgu_ref, bd_ref, h_hbm, wgu_hbm, wd_hbm, o_ref,
                   wgu_buf, wd_buf, st_gu, st_d, x_buf, sem_gu, sem_d, sem_x, *, layer):
    i = pl.program_id(0)

    def row_copy(t, r, xs):
        return pltpu.make_async_copy(h_hbm.at[pl.ds(t, 1), :], x_buf.at[xs, pl.ds(r, 1), :],
                                     sem_x.at[xs])

    def start_rows(blk, xs):
        base = blk * EXPERT_BM

        def body(r, carry):
            row_copy(tok_ref[base + r], r, xs).start()
            return carry

        lax.fori_loop(0, EXPERT_BM, body, 0, unroll=GATHER_UNROLL)

    def wait_rows(xs):
        def body(r, carry):
            row_copy(0, r, xs).wait()
            return carry

        lax.fori_loop(0, EXPERT_BM, body, 0, unroll=GATHER_UNROLL)

    def part_copies(e, c, b):
        gu = pltpu.make_async_copy(wgu_hbm.at[layer, e, pl.ds(c * GU_ROWS, GU_ROWS), :],
                                   st_gu.at[b], sem_gu.at[b])
        dn = pltpu.make_async_copy(wd_hbm.at[layer, e, pl.ds(c * DN_ROWS, DN_ROWS), :],
                                   st_d.at[b], sem_d.at[b])
        return gu, dn

    def start_part(e, c, b):
        gu, dn = part_copies(e, c, b)
        gu.start(priority=1)
        dn.start(priority=1)

    def take_part(e, c, dst):
        b = lax.rem(c, N_STAGE)
        gu, dn = part_copies(e, c, b)
        gu.wait()
        dn.wait()
        wgu_buf[dst, pl.ds(pl.multiple_of(c * GU_ROWS, GU_ROWS), GU_ROWS), :] = (
            st_gu[b].astype(BF16))
        wd_buf[dst, pl.ds(pl.multiple_of(c * DN_ROWS, DN_ROWS), DN_ROWS), :] = (
            st_d[b].astype(BF16))

        @pl.when(c + N_STAGE < W_PARTS)
        def _():
            start_part(e, c + N_STAGE, b)

    @pl.when(i < nv_ref[0])
    def _():
        e = be_ref[i]
        s = slot_ref[i]
        nx = nxt_ref[i]

        xs = lax.rem(i, X_SLOTS)

        @pl.when(i == 0)
        def _():
            for a in range(X_SLOTS - 1):
                @pl.when(a < nv_ref[0])
                def _():
                    start_rows(a, a)
            for b in range(N_STAGE):
                start_part(e, b, b)
            for c in range(W_PARTS):
                take_part(e, jnp.int32(c), s)

        ahead = i + (X_SLOTS - 1)

        @pl.when(ahead < nv_ref[0])
        def _():
            start_rows(ahead, lax.rem(ahead, X_SLOTS))

        @pl.when(jnp.logical_and(first_ref[i] == 1, nx >= 0))
        def _():
            for b in range(N_STAGE):
                start_part(nx, b, b)

        wait_rows(xs)
        x = x_buf[xs].astype(BF16)
        hgu = jnp.dot(x, wgu_buf[s], preferred_element_type=F32) + bgu_ref[0]
        glu = jnp.minimum(hgu[:, :D_EXP], LIMIT)
        lin = jnp.clip(hgu[:, D_EXP:], -LIMIT, LIMIT)
        act = glu * jax.nn.sigmoid(ALPHA * glu) * (lin + 1.0)
        y = jnp.dot(act.astype(BF16), wd_buf[s], preferred_element_type=F32) + bd_ref[0]
        o_ref[...] = y.astype(BF16)

        def body(k, carry):
            take_part(nx, c0_ref[i] + k, 1 - s)
            return carry

        lax.fori_loop(0, nch_ref[i], body, 0)

    @pl.when(i >= nv_ref[0])
    def _():
        o_ref[...] = jnp.zeros(o_ref.shape, BF16)


def _expert_plan(block_e, n_valid, nblk):
    idx = jnp.arange(nblk, dtype=jnp.int32)
    valid = idx < n_valid
    prev_e = jnp.concatenate([block_e[:1], block_e[:-1]])
    first = jnp.logical_and(valid, jnp.logical_or(idx == 0, block_e != prev_e))
    run_id = jnp.cumsum(first.astype(jnp.int32)) - 1
    n_runs = jnp.sum(first.astype(jnp.int32))
    member = jnp.logical_and(run_id[None, :] == idx[:, None], valid[None, :])
    run_len = jnp.sum(member.astype(jnp.int32), axis=1)
    run_first = jnp.min(jnp.where(member, idx[None, :], nblk), axis=1)
    run_e = block_e[jnp.minimum(run_first, nblk - 1)]
    rlen = jnp.maximum(run_len[run_id], 1)
    j = idx - run_first[run_id]
    has_next = jnp.logical_and(valid, run_id + 1 < n_runs)
    nxt = jnp.where(has_next, run_e[jnp.minimum(run_id + 1, nblk - 1)], -1)
    c_lo = (W_PARTS * j) // rlen
    c_hi = (W_PARTS * (j + 1)) // rlen
    nch = jnp.where(has_next, c_hi - c_lo, 0)
    slot = run_id % 2
    cast = lambda a: a.astype(jnp.int32)
    return cast(slot), cast(nxt), cast(first), cast(c_lo), cast(nch)


def experts(block_e, n_valid, tok_buf, hf, wgu, bgu, wd, bd, layer):
    n_rows = tok_buf.shape[0]
    nblk = n_rows // EXPERT_BM
    slot, nxt, first, c0, nch = _expert_plan(block_e, n_valid[0], nblk)

    def bmap(i, *s):
        return (layer * N_EXP + s[0][i], 0, 0)

    any_spec = pl.BlockSpec(memory_space=pl.ANY)
    grid_spec = pltpu.PrefetchScalarGridSpec(
        num_scalar_prefetch=8,
        grid=(nblk,),
        in_specs=[pl.BlockSpec((1, 1, 2 * D_EXP), bmap),
                  pl.BlockSpec((1, 1, D), bmap),
                  any_spec, any_spec, any_spec],
        out_specs=pl.BlockSpec((EXPERT_BM, D), lambda i, *s: (i, 0)),
        scratch_shapes=[pltpu.VMEM((2, D, 2 * D_EXP), BF16),
                        pltpu.VMEM((2, D_EXP, D), BF16),
                        pltpu.VMEM((N_STAGE, GU_ROWS, 2 * D_EXP), F32),
                        pltpu.VMEM((N_STAGE, DN_ROWS, D), F32),
                        pltpu.VMEM((X_SLOTS, EXPERT_BM, D), F32),
                        pltpu.SemaphoreType.DMA((N_STAGE,)),
                        pltpu.SemaphoreType.DMA((N_STAGE,)),
                        pltpu.SemaphoreType.DMA((X_SLOTS,))],
    )
    L = wgu.shape[0]
    return pl.pallas_call(
        functools.partial(_expert_kernel, layer=layer),
        grid_spec=grid_spec,
        out_shape=jax.ShapeDtypeStruct((n_rows, D), BF16),
        compiler_params=_cparams(1, 56),
        name="experts",
    )(block_e, slot, nxt, first, c0, nch, n_valid, tok_buf,
      bgu.reshape(L * N_EXP, 1, 2 * D_EXP), bd.reshape(L * N_EXP, 1, D), hf, wgu, wd)


NEG_BIG = -1e30
RO_E, RO_RANK, RO_GATE = 0, TOP_K, 2 * TOP_K


def _router_kernel(lg_ref, b_ref, tri_ref, ro_ref, cnt_ref, base_ref):
    @pl.when(pl.program_id(0) == 0)
    def _():
        base_ref[...] = jnp.zeros(base_ref.shape, F32)

    lane = lax.broadcasted_iota(jnp.int32, (TILE, ROUTER_PAD), 1)
    lanef = lane.astype(F32)
    lg = jnp.where(lane < N_EXP, lg_ref[...] + b_ref[...], NEG_BIG)
    vals, idxs, hots = [], [], []
    for _ in range(TOP_K):
        m = jnp.max(lg, axis=-1, keepdims=True)
        idx = jnp.min(jnp.where(lg == m, lanef, float(ROUTER_PAD)), axis=-1, keepdims=True)
        hot = lanef == idx
        lg = jnp.where(hot, NEG_BIG, lg)
        vals.append(m)
        idxs.append(idx)
        hots.append(hot)
    ex = [jnp.exp(v - vals[0]) for v in vals]
    den = ex[0] + ex[1] + ex[2] + ex[3]
    chosen = jnp.zeros((TILE, ROUTER_PAD), F32)
    for hot in hots:
        chosen = jnp.where(hot, 1.0, chosen)
    before = jnp.dot(tri_ref[...], chosen.astype(BF16), preferred_element_type=F32)
    tot = before + base_ref[0:1, :]
    rec = jnp.zeros((TILE, ROUTER_PAD), F32)
    for k in range(TOP_K):
        rank = jnp.sum(jnp.where(hots[k], tot, 0.0), axis=-1, keepdims=True)
        rec = jnp.where(lane == RO_E + k, idxs[k], rec)
        rec = jnp.where(lane == RO_RANK + k, rank, rec)
        rec = jnp.where(lane == RO_GATE + k, ex[k] / den, rec)
    ro_ref[...] = rec
    base_ref[...] = base_ref[...] + jnp.sum(chosen, axis=0, keepdims=True)
    cnt_ref[...] = base_ref[...]


def router(logits, b_pad, tri, latent_only):
    if latent_only:
        ntl = SEQ // TILE
        n_tiles = B * ntl
        in_map = lambda i: ((i // ntl) * TPB + 1 + i % ntl, 0)
    else:
        n_tiles = NT
        in_map = lambda i: (i, 0)
    return pl.pallas_call(
        _router_kernel,
        grid=(n_tiles,),
        in_specs=[pl.BlockSpec((TILE, ROUTER_PAD), in_map),
                  _const_spec((1, ROUTER_PAD)), _const_spec((TILE, TILE))],
        out_specs=[_row_spec(ROUTER_PAD), _const_spec((8, ROUTER_PAD))],
        out_shape=[jax.ShapeDtypeStruct((n_tiles * TILE, ROUTER_PAD), F32),
                   jax.ShapeDtypeStruct((8, ROUTER_PAD), F32)],
        scratch_shapes=[pltpu.VMEM((8, ROUTER_PAD), F32)],
        compiler_params=_cparams(1, 32),
        name="router",
    )(logits, b_pad, tri)


def moe(hf, logits, b_router, wgu, bgu, wd, bd, layer, latent_only):
    b_pad = jnp.zeros((1, ROUTER_PAD), F32).at[0, :N_EXP].set(b_router)
    tri = jnp.asarray(np.tril(np.ones((TILE, TILE), np.float32), -1), dtype=BF16)
    ro, cnt = router(logits, b_pad, tri, latent_only)
    n_tok = ro.shape[0]
    if latent_only:
        tok_ids = (jnp.arange(B, dtype=jnp.int32)[:, None] * NB + CTX
                   + jnp.arange(SEQ, dtype=jnp.int32)[None, :]).reshape(-1)
    else:
        tok_ids = jnp.arange(T, dtype=jnp.int32)
    e = ro[:, RO_E:RO_E + TOP_K].astype(jnp.int32)
    rank = ro[:, RO_RANK:RO_RANK + TOP_K].astype(jnp.int32)
    counts = cnt[0, :N_EXP].astype(jnp.int32)
    padded = (counts + EXPERT_BM - 1) // EXPERT_BM * EXPERT_BM
    pends = jnp.cumsum(padded)
    pstarts = pends - padded
    dest = pstarts[e] + rank
    n_pair = n_tok * TOP_K
    nblk = n_pair // EXPERT_BM + N_EXP
    n_rows = nblk * EXPERT_BM
    tok_buf = jnp.zeros((n_rows,), jnp.int32).at[dest.reshape(-1)].set(jnp.repeat(tok_ids, TOP_K))
    n_valid = (pends[-1] // EXPERT_BM).astype(jnp.int32)
    blk = jnp.arange(nblk, dtype=jnp.int32)
    block_e = jnp.minimum(jnp.sum((pends[None, :] <= blk[:, None] * EXPERT_BM).astype(jnp.int32),
                                  axis=1), N_EXP - 1)
    block_e = jnp.where(blk < n_valid, block_e, block_e[n_valid - 1])
    yb = experts(block_e.astype(jnp.int32), n_valid.reshape(1), tok_buf, hf, wgu, bgu, wd, bd,
                 layer)
    ys = [yb[dest[:, k]] for k in range(TOP_K)]
    return ro, ys


def _combine(ro_ref, y_refs):
    ro = ro_ref[...]
    acc = ro[:, RO_GATE:RO_GATE + 1] * y_refs[0][...].astype(F32)
    for k in range(1, TOP_K):
        acc = acc + ro[:, RO_GATE + k:RO_GATE + k + 1] * y_refs[k][...].astype(F32)
    return acc


def _modnorm_moe_kernel(x_ref, ro_ref, y0_ref, y1_ref, y2_ref, y3_ref, g_ref, nw_ref, sh_ref,
                        sc_ref, xo_ref, h_ref):
    x = x_ref[...] + g_ref[0] * _combine(ro_ref, (y0_ref, y1_ref, y2_ref, y3_ref))
    xo_ref[...] = x
    y = _rms(x, nw_ref[...])
    h_ref[...] = (y * (1.0 + sc_ref[0]) + sh_ref[0]).astype(BF16)


def modnorm_moe(x, ro, ys, g, nw, sh, sc):
    return pl.pallas_call(
        _modnorm_moe_kernel,
        grid=(NT,),
        in_specs=[_row_spec(), _row_spec(ROUTER_PAD)] + [_row_spec()] * TOP_K
                 + [_vec_spec(), _const_spec((1, D)), _vec_spec(), _vec_spec()],
        out_specs=[_row_spec(), _row_spec()],
        out_shape=[jax.ShapeDtypeStruct((T, D), F32), jax.ShapeDtypeStruct((T, D), BF16)],
        compiler_params=_cparams(1, 40),
        name="modnorm_moe",
    )(x, ro, *ys, g, nw, sh, sc)


def _final_kernel(x_ref, ro_ref, y0_ref, y1_ref, y2_ref, y3_ref, g_ref, nw_ref, o_ref):
    x = x_ref[...] + g_ref[0] * _combine(ro_ref, (y0_ref, y1_ref, y2_ref, y3_ref))
    o_ref[...] = _rms(x, nw_ref[...])


def final_norm(x, ro, ys, g, nw):
    ntl = SEQ // TILE
    lat = lambda cols: pl.BlockSpec((TILE, cols), lambda b, m: (b * ntl + m, 0))
    return pl.pallas_call(
        _final_kernel,
        grid=(B, ntl),
        in_specs=[pl.BlockSpec((TILE, D), lambda b, m: (b * TPB + 1 + m, 0)),
                  lat(ROUTER_PAD)] + [lat(D)] * TOP_K
                 + [pl.BlockSpec((1, 1, D), lambda b, m: (b, 0, 0)),
                    pl.BlockSpec((1, D), lambda b, m: (0, 0))],
        out_specs=lat(D),
        out_shape=jax.ShapeDtypeStruct((B * SEQ, D), F32),
        compiler_params=_cparams(2, 32),
        name="final_norm",
    )(x, ro, *ys, g, nw)


def _rope_tables():
    t = np.arange(SEQ)
    row = (t // GRID_W).astype(np.float64)
    col = (t % GRID_W).astype(np.float64)
    half = QK_ROPE // 2
    inv = ROPE_BASE ** (-np.arange(0, half, 2, dtype=np.float64) / half)
    ang = np.concatenate([row[:, None] * inv, col[:, None] * inv], axis=-1)
    cos, sin = np.cos(ang), np.sin(ang)
    lat = np.concatenate([cos, cos, -sin, sin], axis=-1)
    ctx = np.concatenate([np.ones((CTX, QK_ROPE)), np.zeros((CTX, QK_ROPE))], axis=-1)
    tab = np.concatenate([ctx, lat], axis=0).astype(np.float32)
    return jnp.asarray(tab), jnp.asarray(tab * np.float32(QK_SCALE))


def _swap_halves(w):
    half = QK_ROPE // 2
    return jnp.concatenate([w[..., half:], w[..., :half]], axis=-1)


IN_O1 = Q_LORA + KV_LORA
IN_O2 = IN_O1 + QK_ROPE
IN_O3 = IN_O2 + 2 * CONV_CH
IN_O4 = IN_O3 + FCH
IN_COLS = IN_O4 + 3 * D
QKV_COLS = IN_O1 + 2 * QK_ROPE
PREP_TR = 256


def _prep_kernel(w_ref, qkv_ref, u_ref, f_ref, gt_ref):
    qkv_ref[:, :IN_O1] = w_ref[0, :, :IN_O1].astype(BF16)
    kb = w_ref[0, :, IN_O1:IN_O1 + 128]
    lane = lax.broadcasted_iota(jnp.int32, kb.shape, 1)
    half = QK_ROPE // 2
    swapped = jnp.where(lane < QK_ROPE, kb,
                        jnp.where(lane < QK_ROPE + half, pltpu.roll(kb, half, 1),
                                  pltpu.roll(kb, QK_ROPE + half, 1)))
    qkv_ref[:, IN_O1:] = swapped.astype(BF16)
    u_ref[...] = w_ref[0, :, IN_O2:IN_O3].astype(BF16)
    f_ref[...] = w_ref[0, :, IN_O3:IN_O4].astype(BF16)
    gt_ref[...] = w_ref[0, :, IN_O4:].astype(BF16)


def prep_w_in(w_in, l):
    widths = (QKV_COLS, 2 * CONV_CH, FCH, 3 * D)
    return pl.pallas_call(
        _prep_kernel,
        grid=(D // PREP_TR,),
        in_specs=[pl.BlockSpec((1, PREP_TR, IN_COLS), lambda i: (l, i, 0))],
        out_specs=[pl.BlockSpec((PREP_TR, n), lambda i: (i, 0)) for n in widths],
        out_shape=[jax.ShapeDtypeStruct((D, n), BF16) for n in widths],
        compiler_params=_cparams(1, 48),
        name="prep_w_in",
    )(w_in)


def _layer_weights(w_uq, w_ukv):
    wq = w_uq.reshape(Q_LORA, H, QK_NOPE + QK_ROPE)
    wq_r = wq[:, :, QK_NOPE:]
    wq = jnp.concatenate([wq[:, :, :QK_NOPE], wq_r, _swap_halves(wq_r)], axis=-1)
    wq = jnp.transpose(wq, (1, 0, 2)).astype(BF16)
    wkv = jnp.transpose(w_ukv.reshape(KV_LORA, H, QK_NOPE + V_DIM), (1, 0, 2)).astype(BF16)
    return wq, wkv


def kernel(x, c, ctx, c_ctx, w_ada, b_ada, norm1, w_in, q_norm, kv_norm, w_uq, w_ukv, w_mla_out,
           conv_dw, conv_dw_b, conv_ln_g, conv_ln_b, w_conv_out, w_four_out, w_out, norm2,
           w_router, b_router, w_gate_up, b_gate_up, w_down, b_down, norm_final):
    L = w_ada.shape[0]
    xt = jnp.concatenate([ctx, x], axis=1).reshape(T, D)
    cc = jnp.zeros((8, D), F32).at[:B].set(c).at[B].set(c_ctx)
    mod = ada_mod(cc, w_ada, b_ada).reshape(L, 8, 6, 1, D)
    tabk, tabq = _rope_tables()
    a_lat, a_ctx, cs_ch = _dft_tables()

    ro = ys = None
    g2_prev = None
    for l in range(L):
        sh1, sc1, g1, sh2, sc2, g2 = [mod[l, :, k] for k in range(6)]
        w_qkv, w_u, w_f, w_gt = prep_w_in(w_in, l)
        wq, wkv = _layer_weights(w_uq[l], w_ukv[l])
        nw1 = norm1[l].reshape(1, D)
        if l == 0:
            h = modnorm(xt, nw1, sh1, sc1)
        else:
            xt, h = modnorm_moe(xt, ro, ys, g2_prev, nw1, sh1, sc1)

        qkv = proj(_proj_plain_kernel, h, w_qkv, QKV_COLS, QKV_COLS, "proj_qkv")
        z = proj(_proj_glu_kernel, h, w_u, CONV_CH, 2 * CONV_CH, "proj_glu")
        fc, fs = proj(_proj_four_kernel, h, w_f, FCH, FCH, "proj_four", extra=(cs_ch,),
                      n_outputs=2)
        gs = proj(_proj_sigmoid_kernel, h, w_gt, 3 * D, D, "proj_gate")

        q = q_proj(qkv, q_norm[l].reshape(1, Q_LORA), wq, tabq)
        k, v = kv_proj(qkv, kv_norm[l].reshape(1, KV_LORA), wkv, tabk)
        a = attention(q, k, v)

        w_pad = jnp.zeros((32, CONV_CH), F32).at[:CONV_W].set(conv_dw[l])
        cv = conv_module(z, w_pad, conv_dw_b[l].reshape(1, CONV_CH),
                         conv_ln_g[l].reshape(1, CONV_CH), conv_ln_b[l].reshape(1, CONV_CH))
        fo = fourier(fc, fs, a_lat, a_ctx)

        m = merge1(a, cv, fo, gs, w_mla_out[l].astype(BF16), w_conv_out[l].astype(BF16),
                   w_four_out[l].astype(BF16))
        wr = jnp.zeros((D, ROUTER_PAD), BF16).at[:, :N_EXP].set(w_router[l].astype(BF16))
        xt, h2, logits = merge2(m, w_out[l].astype(BF16), xt, g1, norm2[l].reshape(1, D), sh2, sc2,
                                wr)
        last = l == L - 1
        ro, ys = moe(h2, logits, b_router[l], w_gate_up, b_gate_up, w_down, b_down, l, last)
        g2_prev = g2

    out = final_norm(xt, ro, ys, g2_prev, norm_final.reshape(1, D))
    return out.reshape(B, SEQ, D)
```

```python
import functools

import numpy as np
import jax
import jax.numpy as jnp
from jax import lax
from jax.experimental import pallas as pl
from jax.experimental.pallas import tpu as pltpu

F32 = jnp.float32
BF16 = jnp.bfloat16

D = 2048
B = 4
SEQ = 2048
CTX = 256
NB = CTX + SEQ
T = B * NB
TILE = 256
TPB = NB // TILE
NT = T // TILE
GRID_W = 64
H = 16
Q_LORA = 512
KV_LORA = 512
QK_NOPE = 128
QK_ROPE = 64
V_DIM = 128
CONV_CH = 1024
CONV_W = 31
FG = 4
FGC = 256
FCH = FG * FGC
N_EXP = 32
TOP_K = 4
D_EXP = 1024
ALPHA = 1.702
LIMIT = 7.0
EPS = 1e-6
ROPE_BASE = 10000.0
QK_SCALE = float((QK_NOPE + QK_ROPE) ** -0.5)
EXPERT_BM = 256
ROUTER_PAD = 128

ARB = "arbitrary"


def _cparams(n_axes, vmem_mb):
    return pltpu.CompilerParams(dimension_semantics=(ARB,) * n_axes,
                                vmem_limit_bytes=vmem_mb << 20)


def _mod_row(i):
    return jnp.where(i % TPB == 0, B, i // TPB)


def _ada_kernel(c_ref, w_ref, b_ref, o_ref):
    c = c_ref[...]
    s = (c * jax.nn.sigmoid(c)).astype(BF16)
    o_ref[0] = jnp.dot(s, w_ref[0].astype(BF16), preferred_element_type=F32) + b_ref[0]


def ada_mod(cc, w_ada, b_ada):
    L, _, N = w_ada.shape
    tn = 1024
    return pl.pallas_call(
        _ada_kernel,
        grid=(L, N // tn),
        in_specs=[pl.BlockSpec((8, D), lambda l, j: (0, 0)),
                  pl.BlockSpec((1, D, tn), lambda l, j: (l, 0, j)),
                  pl.BlockSpec((1, 1, tn), lambda l, j: (l, 0, j))],
        out_specs=pl.BlockSpec((1, 8, tn), lambda l, j: (l, 0, j)),
        out_shape=jax.ShapeDtypeStruct((L, 8, N), F32),
        compiler_params=_cparams(2, 40),
        name="ada_mod",
    )(cc, w_ada, b_ada.reshape(L, 1, N))


def _rms(x, w):
    ms = jnp.mean(x * x, axis=-1, keepdims=True)
    return x * lax.rsqrt(ms + EPS) * w


def _modnorm_kernel(x_ref, nw_ref, sh_ref, sc_ref, h_ref):
    y = _rms(x_ref[...], nw_ref[...])
    h_ref[...] = (y * (1.0 + sc_ref[0]) + sh_ref[0]).astype(BF16)


def _vec_spec():
    return pl.BlockSpec((1, 1, D), lambda i: (_mod_row(i), 0, 0))


def _row_spec(cols=D):
    return pl.BlockSpec((TILE, cols), lambda i: (i, 0))


def _const_spec(shape):
    return pl.BlockSpec(shape, lambda i: (0,) * len(shape))


def modnorm(x, nw, sh, sc):
    return pl.pallas_call(
        _modnorm_kernel,
        grid=(NT,),
        in_specs=[_row_spec(), _const_spec((1, D)), _vec_spec(), _vec_spec()],
        out_specs=_row_spec(),
        out_shape=jax.ShapeDtypeStruct((T, D), BF16),
        compiler_params=_cparams(1, 32),
        name="modnorm",
    )(x, nw, sh, sc)


PROJ_TM = 768


def _proj_plain_kernel(a_ref, w_ref, o_ref):
    o_ref[...] = jnp.dot(a_ref[...], w_ref[...], preferred_element_type=F32).astype(o_ref.dtype)


def _proj_sigmoid_kernel(a_ref, w_ref, o_ref):
    acc = jnp.dot(a_ref[...], w_ref[...], preferred_element_type=F32)
    o_ref[...] = jax.nn.sigmoid(acc).astype(o_ref.dtype)


def _proj_glu_kernel(a_ref, w_ref, o_ref):
    acc = jnp.dot(a_ref[...], w_ref[...], preferred_element_type=F32)
    o_ref[...] = (acc[:, :CONV_CH] * jax.nn.sigmoid(acc[:, CONV_CH:])).astype(o_ref.dtype)


def _proj_four_kernel(a_ref, w_ref, cs_ref, fc_ref, fs_ref):
    f = jnp.dot(a_ref[...], w_ref[...], preferred_element_type=F32).astype(BF16)
    cs = cs_ref[...].astype(BF16)
    for g in range(FG):
        r = jnp.dot(f[:, g * FGC:(g + 1) * FGC], cs, preferred_element_type=F32)
        fc_ref[:, g * FGC:(g + 1) * FGC] = r[:, :FGC].astype(BF16)
        fs_ref[:, g * FGC:(g + 1) * FGC] = r[:, FGC:].astype(BF16)


def proj(kernel, h, w, n_out, tn, name, extra=(), n_outputs=1, vmem_mb=48):
    K, N = w.shape
    tm = PROJ_TM
    tn_out = n_out // (N // tn)
    in_specs = [pl.BlockSpec((tm, K), lambda j, i: (i, 0)),
                pl.BlockSpec((K, tn), lambda j, i: (0, j))]
    for e in extra:
        in_specs.append(pl.BlockSpec(e.shape, lambda j, i, nd=e.ndim: (0,) * nd))
    out_spec = pl.BlockSpec((tm, tn_out), lambda j, i: (i, j))
    out_shape = jax.ShapeDtypeStruct((T, n_out), BF16)
    if n_outputs > 1:
        out_spec = [out_spec] * n_outputs
        out_shape = [out_shape] * n_outputs
    return pl.pallas_call(
        kernel,
        grid=(N // tn, T // tm),
        in_specs=in_specs,
        out_specs=out_spec,
        out_shape=out_shape,
        compiler_params=_cparams(2, vmem_mb),
        name=name,
    )(h, w, *extra)


MLA_TM = 768
MLA_HG = 4


def _lane_lt64(shape):
    return lax.broadcasted_iota(jnp.int32, shape, 1) < QK_ROPE


def _qproj_kernel(cq_ref, nw_ref, w_ref, tab_ref, q_ref, cqn_ref):
    @pl.when(pl.program_id(1) == 0)
    def _():
        cqn_ref[...] = _rms(cq_ref[...].astype(F32), nw_ref[...]).astype(BF16)

    for hh in range(MLA_HG):
        y = jnp.dot(cqn_ref[...], w_ref[hh], preferred_element_type=F32)
        a = y[:, QK_NOPE:] * tab_ref[...]
        q_ref[hh, :, :QK_NOPE] = (y[:, :QK_NOPE] * QK_SCALE).astype(BF16)
        q_ref[hh, :, QK_NOPE:] = (a + pltpu.roll(a, QK_ROPE, 1)).astype(BF16)


def q_proj(qkv, nw, wq, tabq):
    tm = MLA_TM
    return pl.pallas_call(
        _qproj_kernel,
        grid=(T // tm, H // MLA_HG),
        in_specs=[pl.BlockSpec((tm, Q_LORA), lambda i, h: (i, 0)),
                  pl.BlockSpec((1, Q_LORA), lambda i, h: (0, 0)),
                  pl.BlockSpec((MLA_HG, Q_LORA, 256), lambda i, h: (h, 0, 0)),
                  pl.BlockSpec((tm, 128), lambda i, h: (i % (NB // tm), 0))],
        out_specs=pl.BlockSpec((MLA_HG, tm, 256), lambda i, h: (h, i, 0)),
        out_shape=jax.ShapeDtypeStruct((H, T, 256), BF16),
        scratch_shapes=[pltpu.VMEM((tm, Q_LORA), BF16)],
        compiler_params=_cparams(2, 32),
        name="q_proj",
    )(qkv, nw, wq, tabq)


def _kvproj_kernel(ckv_ref, kr_ref, nw_ref, w_ref, tab_ref, k_ref, v_ref, ckvn_ref, k2_ref):
    @pl.when(pl.program_id(1) == 0)
    def _():
        ckvn_ref[...] = _rms(ckv_ref[...].astype(F32), nw_ref[...]).astype(BF16)
        a = kr_ref[...].astype(F32) * tab_ref[...]
        s = a + pltpu.roll(a, QK_ROPE, 1)
        k2_ref[...] = jnp.where(_lane_lt64(s.shape), s, 0.0).astype(BF16)

    for hh in range(MLA_HG):
        y = jnp.dot(ckvn_ref[...], w_ref[hh], preferred_element_type=F32)
        k_ref[hh, :, :QK_NOPE] = y[:, :QK_NOPE].astype(BF16)
        k_ref[hh, :, QK_NOPE:] = k2_ref[...]
        v_ref[hh] = y[:, QK_NOPE:].astype(BF16)


def kv_proj(qkv, nw, wkv, tabk):
    tm = MLA_TM
    return pl.pallas_call(
        _kvproj_kernel,
        grid=(T // tm, H // MLA_HG),
        in_specs=[pl.BlockSpec((tm, KV_LORA), lambda i, h: (i, 1)),
                  pl.BlockSpec((tm, 128), lambda i, h: (i, (Q_LORA + KV_LORA) // 128)),
                  pl.BlockSpec((1, KV_LORA), lambda i, h: (0, 0)),
                  pl.BlockSpec((MLA_HG, KV_LORA, 256), lambda i, h: (h, 0, 0)),
                  pl.BlockSpec((tm, 128), lambda i, h: (i % (NB // tm), 0))],
        out_specs=[pl.BlockSpec((MLA_HG, tm, 256), lambda i, h: (h, i, 0)),
                   pl.BlockSpec((MLA_HG, tm, V_DIM), lambda i, h: (h, i, 0))],
        out_shape=[jax.ShapeDtypeStruct((H, T, 256), BF16),
                   jax.ShapeDtypeStruct((H, T, V_DIM), BF16)],
        scratch_shapes=[pltpu.VMEM((tm, KV_LORA), BF16), pltpu.VMEM((tm, 128), BF16)],
        compiler_params=_cparams(2, 32),
        name="kv_proj",
    )(qkv, qkv, nw, wkv, tabk)


ATT_TQ = 256


def _attend(q, k, v):
    s = lax.dot_general(q, k, (((1,), (1,)), ((), ())), preferred_element_type=F32)
    m = jnp.max(s, axis=-1, keepdims=True)
    p = jnp.exp(s - m)
    l = jnp.sum(p, axis=-1, keepdims=True)
    o = jnp.dot(p.astype(BF16), v, preferred_element_type=F32)
    return (o / l).astype(BF16)


def _attn_kernel(q_ref, k_ref, v_ref, o_ref):
    o_ref[:CTX, :] = _attend(q_ref[0, :CTX, :], k_ref[0, :CTX, :], v_ref[0, :CTX, :])
    for c in range(SEQ // ATT_TQ):
        r0 = CTX + c * ATT_TQ
        o_ref[r0:r0 + ATT_TQ, :] = _attend(q_ref[0, r0:r0 + ATT_TQ, :], k_ref[0], v_ref[0])


def attention(q, k, v):
    return pl.pallas_call(
        _attn_kernel,
        grid=(B, H),
        in_specs=[pl.BlockSpec((1, NB, 256), lambda b, h: (h, b, 0)),
                  pl.BlockSpec((1, NB, 256), lambda b, h: (h, b, 0)),
                  pl.BlockSpec((1, NB, V_DIM), lambda b, h: (h, b, 0))],
        out_specs=pl.BlockSpec((NB, V_DIM), lambda b, h: (b, h)),
        out_shape=jax.ShapeDtypeStruct((T, H * V_DIM), BF16),
        compiler_params=_cparams(2, 48),
        name="attention",
    )(q, k, v)


HALO = 16
CONV_RC = 32
SUBLANES = 8
CONV_SH_ROWS = TILE + 2 * HALO - SUBLANES


def _conv_kernel(zp_ref, zc_ref, zn_ref, w_ref, b_ref, g_ref, be_ref, o_ref, buf_ref, sh_ref):
    j = pl.program_id(0) % TPB
    prev_ok = j >= 2
    next_ok = jnp.logical_and(j >= 1, j <= TPB - 2)
    buf_ref[0:HALO, :] = jnp.where(prev_ok, zp_ref[...].astype(F32), 0.0)
    buf_ref[HALO:HALO + TILE, :] = zc_ref[...].astype(F32)
    buf_ref[HALO + TILE:, :] = jnp.where(next_ok, zn_ref[...].astype(F32), 0.0)
    for s in range(1, SUBLANES):
        sh_ref[s - 1] = buf_ref[s:s + CONV_SH_ROWS, :]
    off = HALO - CONV_W // 2
    for rc in range(TILE // CONV_RC):
        r0 = rc * CONV_RC
        acc = jnp.zeros((CONV_RC, CONV_CH), F32) + b_ref[...]
        for t in range(CONV_W):
            q, s = divmod(off + t, SUBLANES)
            a0 = r0 + q * SUBLANES
            if s == 0:
                tap = buf_ref[a0:a0 + CONV_RC, :]
            else:
                tap = sh_ref[s - 1, a0:a0 + CONV_RC, :]
            acc = acc + tap * w_ref[t:t + 1, :]
        mu = jnp.mean(acc, axis=-1, keepdims=True)
        d = acc - mu
        var = jnp.mean(d * d, axis=-1, keepdims=True)
        zn = d * lax.rsqrt(var + EPS) * g_ref[...] + be_ref[...]
        o_ref[r0:r0 + CONV_RC, :] = (zn * jax.nn.sigmoid(zn)).astype(BF16)


def conv_module(z, w_pad, b, g, be):
    nh = T // HALO
    return pl.pallas_call(
        _conv_kernel,
        grid=(NT,),
        in_specs=[pl.BlockSpec((HALO, CONV_CH),
                               lambda i: (jnp.maximum(i * (TILE // HALO) - 1, 0), 0)),
                  pl.BlockSpec((TILE, CONV_CH), lambda i: (i, 0)),
                  pl.BlockSpec((HALO, CONV_CH),
                               lambda i: (jnp.minimum((i + 1) * (TILE // HALO), nh - 1), 0)),
                  _const_spec((32, CONV_CH)), _const_spec((1, CONV_CH)),
                  _const_spec((1, CONV_CH)), _const_spec((1, CONV_CH))],
        out_specs=pl.BlockSpec((TILE, CONV_CH), lambda i: (i, 0)),
        out_shape=jax.ShapeDtypeStruct((T, CONV_CH), BF16),
        scratch_shapes=[pltpu.VMEM((TILE + 2 * HALO, CONV_CH), F32),
                        pltpu.VMEM((SUBLANES - 1, CONV_SH_ROWS, CONV_CH), F32)],
        compiler_params=_cparams(1, 40),
        name="conv_module",
    )(z, z, z, w_pad, b, g, be)


def _dft_tables():
    def cs(n):
        k = np.arange(n, dtype=np.int64)
        ang = 2.0 * np.pi * ((k[:, None] * k[None, :]) % n).astype(np.float64) / n
        return np.cos(ang), np.sin(ang)

    cl, sl = cs(SEQ)
    cc, sc = cs(CTX)
    cg, sg = cs(FGC)
    a_lat = np.concatenate([cl, -sl], axis=1).astype(np.float32)
    a_ctx = np.concatenate([cc, -sc], axis=1).astype(np.float32)
    cs_ch = np.concatenate([cg, sg], axis=1).astype(np.float32)
    return jnp.asarray(a_lat), jnp.asarray(a_ctx), jnp.asarray(cs_ch)


LAT_SCALE = float((SEQ * FGC) ** -0.5)
CTX_SCALE = float((CTX * FGC) ** -0.5)


def _fourier_kernel(al_ref, ac_ref, fc_ref, fs_ref, o_ref):
    m = pl.program_id(1)

    @pl.when(m == 0)
    def _():
        r = jnp.dot(ac_ref[:, :CTX].astype(BF16), fc_ref[0, :CTX, :], preferred_element_type=F32)
        r = r + jnp.dot(ac_ref[:, CTX:].astype(BF16), fs_ref[0, :CTX, :],
                        preferred_element_type=F32)
        o_ref[...] = (r * CTX_SCALE).astype(BF16)

    @pl.when(m > 0)
    def _():
        r = jnp.dot(al_ref[:, :SEQ].astype(BF16), fc_ref[0, CTX:, :], preferred_element_type=F32)
        r = r + jnp.dot(al_ref[:, SEQ:].astype(BF16), fs_ref[0, CTX:, :],
                        preferred_element_type=F32)
        o_ref[...] = (r * LAT_SCALE).astype(BF16)


def fourier(fc, fs, a_lat, a_ctx):
    return pl.pallas_call(
        _fourier_kernel,
        grid=(B, TPB),
        in_specs=[pl.BlockSpec((TILE, 2 * SEQ), lambda b, m: (jnp.maximum(m - 1, 0), 0)),
                  pl.BlockSpec((CTX, 2 * CTX), lambda b, m: (0, 0)),
                  pl.BlockSpec((1, NB, FCH), lambda b, m: (b, 0, 0)),
                  pl.BlockSpec((1, NB, FCH), lambda b, m: (b, 0, 0))],
        out_specs=pl.BlockSpec((TILE, FCH), lambda b, m: (b * TPB + m, 0)),
        out_shape=jax.ShapeDtypeStruct((T, FCH), BF16),
        compiler_params=_cparams(2, 48),
        name="fourier",
    )(a_lat, a_ctx, fc.reshape(B, NB, FCH), fs.reshape(B, NB, FCH))


MERGE_TM = 512
MERGE_TN = 1024


def _merge1_kernel(a_ref, cv_ref, fo_ref, g0_ref, g1_ref, g2_ref, wm_ref, wc_ref, wf_ref, o_ref):
    m = g0_ref[...].astype(F32) * jnp.dot(a_ref[...], wm_ref[...], preferred_element_type=F32)
    m = m + g1_ref[...].astype(F32) * jnp.dot(cv_ref[...], wc_ref[...],
                                               preferred_element_type=F32)
    m = m + g2_ref[...].astype(F32) * jnp.dot(fo_ref[...], wf_ref[...],
                                               preferred_element_type=F32)
    o_ref[...] = m.astype(BF16)


def merge1(a, cv, fo, gs, wm, wc, wf):
    tm, tn = MERGE_TM, MERGE_TN
    nj = D // tn

    def gspec(k):
        return pl.BlockSpec((tm, tn), lambda j, i: (i, k * nj + j))

    return pl.pallas_call(
        _merge1_kernel,
        grid=(nj, T // tm),
        in_specs=[pl.BlockSpec((tm, H * V_DIM), lambda j, i: (i, 0)),
                  pl.BlockSpec((tm, CONV_CH), lambda j, i: (i, 0)),
                  pl.BlockSpec((tm, FCH), lambda j, i: (i, 0)),
                  gspec(0), gspec(1), gspec(2),
                  pl.BlockSpec((H * V_DIM, tn), lambda j, i: (0, j)),
                  pl.BlockSpec((CONV_CH, tn), lambda j, i: (0, j)),
                  pl.BlockSpec((FCH, tn), lambda j, i: (0, j))],
        out_specs=pl.BlockSpec((tm, tn), lambda j, i: (i, j)),
        out_shape=jax.ShapeDtypeStruct((T, D), BF16),
        compiler_params=_cparams(2, 48),
        name="merge1",
    )(a, cv, fo, gs, gs, gs, wm, wc, wf)


def _merge2_kernel(m_ref, w_ref, x_ref, g_ref, nw_ref, sh_ref, sc_ref, wr_ref,
                   xo_ref, h_ref, lg_ref):
    acc = jnp.dot(m_ref[...], w_ref[...], preferred_element_type=F32)
    x = x_ref[...] + g_ref[0] * acc
    xo_ref[...] = x
    h = _rms(x, nw_ref[...]) * (1.0 + sc_ref[0]) + sh_ref[0]
    h_ref[...] = h
    lg_ref[...] = jnp.dot(h.astype(BF16), wr_ref[...], preferred_element_type=F32)


def merge2(m, w_out, x, g, nw, sh, sc, wr):
    return pl.pallas_call(
        _merge2_kernel,
        grid=(NT,),
        in_specs=[_row_spec(), _const_spec((D, D)), _row_spec(), _vec_spec(),
                  _const_spec((1, D)), _vec_spec(), _vec_spec(), _const_spec((D, ROUTER_PAD))],
        out_specs=[_row_spec(), _row_spec(), _row_spec(ROUTER_PAD)],
        out_shape=[jax.ShapeDtypeStruct((T, D), F32), jax.ShapeDtypeStruct((T, D), F32),
                   jax.ShapeDtypeStruct((T, ROUTER_PAD), F32)],
        compiler_params=_cparams(1, 52),
        name="merge2",
    )(m, w_out, x, g, nw, sh, sc, wr)


W_PARTS = 8
GU_ROWS = D // W_PARTS
DN_ROWS = D_EXP // W_PARTS
N_STAGE = 2
GATHER_UNROLL = 8
X_SLOTS = 3
ISSUE_GROUPS = 4


def _expert_kernel(be_ref, slot_ref, nxt_ref, first_ref, c0_ref, nch_ref, nv_ref, tok_ref,
                   bgu_ref, bd_ref, h_hbm, wgu_hbm, wd_hbm, o_ref,
                   wgu_buf, wd_buf, st_gu, st_d, x_buf, sem_gu, sem_d, sem_x, *, layer):
    i = pl.program_id(0)

    def row_copy(t, r, xs):
        return pltpu.make_async_copy(h_hbm.at[pl.ds(t, 1), :], x_buf.at[xs, pl.ds(r, 1), :],
                                     sem_x.at[xs])

    def start_rows(blk, xs):
        base = blk * EXPERT_BM

        def body(r, carry):
            row_copy(tok_ref[base + r], r, xs).start()
            return carry

        lax.fori_loop(0, EXPERT_BM, body, 0, unroll=GATHER_UNROLL)

    def wait_rows(xs):
        def body(r, carry):
            row_copy(0, r, xs).wait()
            return carry

        lax.fori_loop(0, EXPERT_BM, body, 0, unroll=GATHER_UNROLL)

    def part_copies(e, c, b):
        gu = pltpu.make_async_copy(wgu_hbm.at[layer, e, pl.ds(c * GU_ROWS, GU_ROWS), :],
                                   st_gu.at[b], sem_gu.at[b])
        dn = pltpu.make_async_copy(wd_hbm.at[layer, e, pl.ds(c * DN_ROWS, DN_ROWS), :],
                                   st_d.at[b], sem_d.at[b])
        return gu, dn

    def start_part(e, c, b):
        gu, dn = part_copies(e, c, b)
        gu.start(priority=1)
        dn.start(priority=1)

    def take_part(e, c, dst):
        b = lax.rem(c, N_STAGE)
        gu, dn = part_copies(e, c, b)
        gu.wait()
        dn.wait()
        wgu_buf[dst, pl.ds(pl.multiple_of(c * GU_ROWS, GU_ROWS), GU_ROWS), :] = (
            st_gu[b].astype(BF16))
        wd_buf[dst, pl.ds(pl.multiple_of(c * DN_ROWS, DN_ROWS), DN_ROWS), :] = (
            st_d[b].astype(BF16))

        @pl.when(c + N_STAGE < W_PARTS)
        def _():
            start_part(e, c + N_STAGE, b)

    @pl.when(i < nv_ref[0])
    def _():
        e = be_ref[i]
        s = slot_ref[i]
        nx = nxt_ref[i]

        xs = lax.rem(i, X_SLOTS)

        @pl.when(i == 0)
        def _():
            for a in range(X_SLOTS - 1):
                @pl.when(a < nv_ref[0])
                def _():
                    start_rows(a, a)
            for b in range(N_STAGE):
                start_part(e, b, b)
            for c in range(W_PARTS):
                take_part(e, jnp.int32(c), s)

        @pl.when(jnp.logical_and(first_ref[i] == 1, nx >= 0))
        def _():
            for b in range(N_STAGE):
                start_part(nx, b, b)

        wait_rows(xs)

        ahead = i + (X_SLOTS - 1)
        axs = lax.rem(ahead, X_SLOTS)

        def mlp(fetch_ahead):
            x = x_buf[xs].astype(BF16)
            slab = 2 * D_EXP // ISSUE_GROUPS
            per = EXPERT_BM // ISSUE_GROUPS
            parts = []
            for q in range(ISSUE_GROUPS):
                if fetch_ahead:
                    for r in range(q * per, (q + 1) * per):
                        row_copy(tok_ref[ahead * EXPERT_BM + r], r, axs).start()
                parts.append(jnp.dot(x, wgu_buf[s, :, q * slab:(q + 1) * slab],
                                     preferred_element_type=F32)
                             + bgu_ref[0, :, q * slab:(q + 1) * slab])
            hgu = jnp.concatenate(parts, axis=1)
            glu = jnp.minimum(hgu[:, :D_EXP], LIMIT)
            lin = jnp.clip(hgu[:, D_EXP:], -LIMIT, LIMIT)
            act = glu * jax.nn.sigmoid(ALPHA * glu) * (lin + 1.0)
            y = jnp.dot(act.astype(BF16), wd_buf[s], preferred_element_type=F32) + bd_ref[0]
            o_ref[...] = y.astype(BF16)

        @pl.when(ahead < nv_ref[0])
        def _():
            mlp(True)

        @pl.when(ahead >= nv_ref[0])
        def _():
            mlp(False)

        def body(k, carry):
            take_part(nx, c0_ref[i] + k, 1 - s)
            return carry

        lax.fori_loop(0, nch_ref[i], body, 0)

    @pl.when(i >= nv_ref[0])
    def _():
        o_ref[...] = jnp.zeros(o_ref.shape, BF16)


def _expert_plan(block_e, n_valid, nblk):
    idx = jnp.arange(nblk, dtype=jnp.int32)
    valid = idx < n_valid
    prev_e = jnp.concatenate([block_e[:1], block_e[:-1]])
    first = jnp.logical_and(valid, jnp.logical_or(idx == 0, block_e != prev_e))
    run_id = jnp.cumsum(first.astype(jnp.int32)) - 1
    n_runs = jnp.sum(first.astype(jnp.int32))
    member = jnp.logical_and(run_id[None, :] == idx[:, None], valid[None, :])
    run_len = jnp.sum(member.astype(jnp.int32), axis=1)
    run_first = jnp.min(jnp.where(member, idx[None, :], nblk), axis=1)
    run_e = block_e[jnp.minimum(run_first, nblk - 1)]
    rlen = jnp.maximum(run_len[run_id], 1)
    j = idx - run_first[run_id]
    has_next = jnp.logical_and(valid, run_id + 1 < n_runs)
    nxt = jnp.where(has_next, run_e[jnp.minimum(run_id + 1, nblk - 1)], -1)
    c_lo = (W_PARTS * j) // rlen
    c_hi = (W_PARTS * (j + 1)) // rlen
    nch = jnp.where(has_next, c_hi - c_lo, 0)
    slot = run_id % 2
    cast = lambda a: a.astype(jnp.int32)
    return cast(slot), cast(nxt), cast(first), cast(c_lo), cast(nch)


def experts(block_e, n_valid, tok_buf, hf, wgu, bgu, wd, bd, layer):
    n_rows = tok_buf.shape[0]
    nblk = n_rows // EXPERT_BM
    slot, nxt, first, c0, nch = _expert_plan(block_e, n_valid[0], nblk)

    def bmap(i, *s):
        return (layer * N_EXP + s[0][i], 0, 0)

    any_spec = pl.BlockSpec(memory_space=pl.ANY)
    grid_spec = pltpu.PrefetchScalarGridSpec(
        num_scalar_prefetch=8,
        grid=(nblk,),
        in_specs=[pl.BlockSpec((1, 1, 2 * D_EXP), bmap),
                  pl.BlockSpec((1, 1, D), bmap),
                  any_spec, any_spec, any_spec],
        out_specs=pl.BlockSpec((EXPERT_BM, D), lambda i, *s: (i, 0)),
        scratch_shapes=[pltpu.VMEM((2, D, 2 * D_EXP), BF16),
                        pltpu.VMEM((2, D_EXP, D), BF16),
                        pltpu.VMEM((N_STAGE, GU_ROWS, 2 * D_EXP), F32),
                        pltpu.VMEM((N_STAGE, DN_ROWS, D), F32),
                        pltpu.VMEM((X_SLOTS, EXPERT_BM, D), F32),
                        pltpu.SemaphoreType.DMA((N_STAGE,)),
                        pltpu.SemaphoreType.DMA((N_STAGE,)),
                        pltpu.SemaphoreType.DMA((X_SLOTS,))],
    )
    L = wgu.shape[0]
    return pl.pallas_call(
        functools.partial(_expert_kernel, layer=layer),
        grid_spec=grid_spec,
        out_shape=jax.ShapeDtypeStruct((n_rows, D), BF16),
        compiler_params=_cparams(1, 56),
        name="experts",
    )(block_e, slot, nxt, first, c0, nch, n_valid, tok_buf,
      bgu.reshape(L * N_EXP, 1, 2 * D_EXP), bd.reshape(L * N_EXP, 1, D), hf, wgu, wd)


NEG_BIG = -1e30
RO_E, RO_RANK, RO_GATE = 0, TOP_K, 2 * TOP_K


def _router_kernel(lg_ref, b_ref, tri_ref, ro_ref, cnt_ref, base_ref):
    @pl.when(pl.program_id(0) == 0)
    def _():
        base_ref[...] = jnp.zeros(base_ref.shape, F32)

    lane = lax.broadcasted_iota(jnp.int32, (TILE, ROUTER_PAD), 1)
    lanef = lane.astype(F32)
    lg = jnp.where(lane < N_EXP, lg_ref[...] + b_ref[...], NEG_BIG)
    vals, idxs, hots = [], [], []
    for _ in range(TOP_K):
        m = jnp.max(lg, axis=-1, keepdims=True)
        idx = jnp.min(jnp.where(lg == m, lanef, float(ROUTER_PAD)), axis=-1, keepdims=True)
        hot = lanef == idx
        lg = jnp.where(hot, NEG_BIG, lg)
        vals.append(m)
        idxs.append(idx)
        hots.append(hot)
    ex = [jnp.exp(v - vals[0]) for v in vals]
    den = ex[0] + ex[1] + ex[2] + ex[3]
    chosen = jnp.zeros((TILE, ROUTER_PAD), F32)
    for hot in hots:
        chosen = jnp.where(hot, 1.0, chosen)
    before = jnp.dot(tri_ref[...], chosen.astype(BF16), preferred_element_type=F32)
    tot = before + base_ref[0:1, :]
    rec = jnp.zeros((TILE, ROUTER_PAD), F32)
    for k in range(TOP_K):
        rank = jnp.sum(jnp.where(hots[k], tot, 0.0), axis=-1, keepdims=True)
        rec = jnp.where(lane == RO_E + k, idxs[k], rec)
        rec = jnp.where(lane == RO_RANK + k, rank, rec)
        rec = jnp.where(lane == RO_GATE + k, ex[k] / den, rec)
    ro_ref[...] = rec
    base_ref[...] = base_ref[...] + jnp.sum(chosen, axis=0, keepdims=True)
    cnt_ref[...] = base_ref[...]


def router(logits, b_pad, tri, latent_only):
    if latent_only:
        ntl = SEQ // TILE
        n_tiles = B * ntl
        in_map = lambda i: ((i // ntl) * TPB + 1 + i % ntl, 0)
    else:
        n_tiles = NT
        in_map = lambda i: (i, 0)
    return pl.pallas_call(
        _router_kernel,
        grid=(n_tiles,),
        in_specs=[pl.BlockSpec((TILE, ROUTER_PAD), in_map),
                  _const_spec((1, ROUTER_PAD)), _const_spec((TILE, TILE))],
        out_specs=[_row_spec(ROUTER_PAD), _const_spec((8, ROUTER_PAD))],
        out_shape=[jax.ShapeDtypeStruct((n_tiles * TILE, ROUTER_PAD), F32),
                   jax.ShapeDtypeStruct((8, ROUTER_PAD), F32)],
        scratch_shapes=[pltpu.VMEM((8, ROUTER_PAD), F32)],
        compiler_params=_cparams(1, 32),
        name="router",
    )(logits, b_pad, tri)


def moe(hf, logits, b_router, wgu, bgu, wd, bd, layer, latent_only):
    b_pad = jnp.zeros((1, ROUTER_PAD), F32).at[0, :N_EXP].set(b_router)
    tri = jnp.asarray(np.tril(np.ones((TILE, TILE), np.float32), -1), dtype=BF16)
    ro, cnt = router(logits, b_pad, tri, latent_only)
    n_tok = ro.shape[0]
    if latent_only:
        tok_ids = (jnp.arange(B, dtype=jnp.int32)[:, None] * NB + CTX
                   + jnp.arange(SEQ, dtype=jnp.int32)[None, :]).reshape(-1)
    else:
        tok_ids = jnp.arange(T, dtype=jnp.int32)
    e = ro[:, RO_E:RO_E + TOP_K].astype(jnp.int32)
    rank = ro[:, RO_RANK:RO_RANK + TOP_K].astype(jnp.int32)
    counts = cnt[0, :N_EXP].astype(jnp.int32)
    padded = (counts + EXPERT_BM - 1) // EXPERT_BM * EXPERT_BM
    pends = jnp.cumsum(padded)
    pstarts = pends - padded
    dest = pstarts[e] + rank
    n_pair = n_tok * TOP_K
    nblk = n_pair // EXPERT_BM + N_EXP
    n_rows = nblk * EXPERT_BM
    tok_buf = jnp.zeros((n_rows,), jnp.int32).at[dest.reshape(-1)].set(jnp.repeat(tok_ids, TOP_K))
    n_valid = (pends[-1] // EXPERT_BM).astype(jnp.int32)
    blk = jnp.arange(nblk, dtype=jnp.int32)
    block_e = jnp.minimum(jnp.sum((pends[None, :] <= blk[:, None] * EXPERT_BM).astype(jnp.int32),
                                  axis=1), N_EXP - 1)
    block_e = jnp.where(blk < n_valid, block_e, block_e[n_valid - 1])
    yb = experts(block_e.astype(jnp.int32), n_valid.reshape(1), tok_buf, hf, wgu, bgu, wd, bd,
                 layer)
    ys = [yb[dest[:, k]] for k in range(TOP_K)]
    return ro, ys


def _combine(ro_ref, y_refs):
    ro = ro_ref[...]
    acc = ro[:, RO_GATE:RO_GATE + 1] * y_refs[0][...].astype(F32)
    for k in range(1, TOP_K):
        acc = acc + ro[:, RO_GATE + k:RO_GATE + k + 1] * y_refs[k][...].astype(F32)
    return acc


def _modnorm_moe_kernel(x_ref, ro_ref, y0_ref, y1_ref, y2_ref, y3_ref, g_ref, nw_ref, sh_ref,
                        sc_ref, xo_ref, h_ref):
    x = x_ref[...] + g_ref[0] * _combine(ro_ref, (y0_ref, y1_ref, y2_ref, y3_ref))
    xo_ref[...] = x
    y = _rms(x, nw_ref[...])
    h_ref[...] = (y * (1.0 + sc_ref[0]) + sh_ref[0]).astype(BF16)


def modnorm_moe(x, ro, ys, g, nw, sh, sc):
    return pl.pallas_call(
        _modnorm_moe_kernel,
        grid=(NT,),
        in_specs=[_row_spec(), _row_spec(ROUTER_PAD)] + [_row_spec()] * TOP_K
                 + [_vec_spec(), _const_spec((1, D)), _vec_spec(), _vec_spec()],
        out_specs=[_row_spec(), _row_spec()],
        out_shape=[jax.ShapeDtypeStruct((T, D), F32), jax.ShapeDtypeStruct((T, D), BF16)],
        compiler_params=_cparams(1, 40),
        name="modnorm_moe",
    )(x, ro, *ys, g, nw, sh, sc)


def _final_kernel(x_ref, ro_ref, y0_ref, y1_ref, y2_ref, y3_ref, g_ref, nw_ref, o_ref):
    x = x_ref[...] + g_ref[0] * _combine(ro_ref, (y0_ref, y1_ref, y2_ref, y3_ref))
    o_ref[...] = _rms(x, nw_ref[...])


def final_norm(x, ro, ys, g, nw):
    ntl = SEQ // TILE
    lat = lambda cols: pl.BlockSpec((TILE, cols), lambda b, m: (b * ntl + m, 0))
    return pl.pallas_call(
        _final_kernel,
        grid=(B, ntl),
        in_specs=[pl.BlockSpec((TILE, D), lambda b, m: (b * TPB + 1 + m, 0)),
                  lat(ROUTER_PAD)] + [lat(D)] * TOP_K
                 + [pl.BlockSpec((1, 1, D), lambda b, m: (b, 0, 0)),
                    pl.BlockSpec((1, D), lambda b, m: (0, 0))],
        out_specs=lat(D),
        out_shape=jax.ShapeDtypeStruct((B * SEQ, D), F32),
        compiler_params=_cparams(2, 32),
        name="final_norm",
    )(x, ro, *ys, g, nw)


def _rope_tables():
    t = np.arange(SEQ)
    row = (t // GRID_W).astype(np.float64)
    col = (t % GRID_W).astype(np.float64)
    half = QK_ROPE // 2
    inv = ROPE_BASE ** (-np.arange(0, half, 2, dtype=np.float64) / half)
    ang = np.concatenate([row[:, None] * inv, col[:, None] * inv], axis=-1)
    cos, sin = np.cos(ang), np.sin(ang)
    lat = np.concatenate([cos, cos, -sin, sin], axis=-1)
    ctx = np.concatenate([np.ones((CTX, QK_ROPE)), np.zeros((CTX, QK_ROPE))], axis=-1)
    tab = np.concatenate([ctx, lat], axis=0).astype(np.float32)
    return jnp.asarray(tab), jnp.asarray(tab * np.float32(QK_SCALE))


def _swap_halves(w):
    half = QK_ROPE // 2
    return jnp.concatenate([w[..., half:], w[..., :half]], axis=-1)


IN_O1 = Q_LORA + KV_LORA
IN_O2 = IN_O1 + QK_ROPE
IN_O3 = IN_O2 + 2 * CONV_CH
IN_O4 = IN_O3 + FCH
IN_COLS = IN_O4 + 3 * D
QKV_COLS = IN_O1 + 2 * QK_ROPE
PREP_TR = 256


def _prep_kernel(w_ref, qkv_ref, u_ref, f_ref, gt_ref):
    qkv_ref[:, :IN_O1] = w_ref[0, :, :IN_O1].astype(BF16)
    kb = w_ref[0, :, IN_O1:IN_O1 + 128]
    lane = lax.broadcasted_iota(jnp.int32, kb.shape, 1)
    half = QK_ROPE // 2
    swapped = jnp.where(lane < QK_ROPE, kb,
                        jnp.where(lane < QK_ROPE + half, pltpu.roll(kb, half, 1),
                                  pltpu.roll(kb, QK_ROPE + half, 1)))
    qkv_ref[:, IN_O1:] = swapped.astype(BF16)
    u_ref[...] = w_ref[0, :, IN_O2:IN_O3].astype(BF16)
    f_ref[...] = w_ref[0, :, IN_O3:IN_O4].astype(BF16)
    gt_ref[...] = w_ref[0, :, IN_O4:].astype(BF16)


def prep_w_in(w_in, l):
    widths = (QKV_COLS, 2 * CONV_CH, FCH, 3 * D)
    return pl.pallas_call(
        _prep_kernel,
        grid=(D // PREP_TR,),
        in_specs=[pl.BlockSpec((1, PREP_TR, IN_COLS), lambda i: (l, i, 0))],
        out_specs=[pl.BlockSpec((PREP_TR, n), lambda i: (i, 0)) for n in widths],
        out_shape=[jax.ShapeDtypeStruct((D, n), BF16) for n in widths],
        compiler_params=_cparams(1, 48),
        name="prep_w_in",
    )(w_in)


def _layer_weights(w_uq, w_ukv):
    wq = w_uq.reshape(Q_LORA, H, QK_NOPE + QK_ROPE)
    wq_r = wq[:, :, QK_NOPE:]
    wq = jnp.concatenate([wq[:, :, :QK_NOPE], wq_r, _swap_halves(wq_r)], axis=-1)
    wq = jnp.transpose(wq, (1, 0, 2)).astype(BF16)
    wkv = jnp.transpose(w_ukv.reshape(KV_LORA, H, QK_NOPE + V_DIM), (1, 0, 2)).astype(BF16)
    return wq, wkv


def kernel(x, c, ctx, c_ctx, w_ada, b_ada, norm1, w_in, q_norm, kv_norm, w_uq, w_ukv, w_mla_out,
           conv_dw, conv_dw_b, conv_ln_g, conv_ln_b, w_conv_out, w_four_out, w_out, norm2,
           w_router, b_router, w_gate_up, b_gate_up, w_down, b_down, norm_final):
    L = w_ada.shape[0]
    xt = jnp.concatenate([ctx, x], axis=1).reshape(T, D)
    cc = jnp.zeros((8, D), F32).at[:B].set(c).at[B].set(c_ctx)
    mod = ada_mod(cc, w_ada, b_ada).reshape(L, 8, 6, 1, D)
    tabk, tabq = _rope_tables()
    a_lat, a_ctx, cs_ch = _dft_tables()

    ro = ys = None
    g2_prev = None
    for l in range(L):
        sh1, sc1, g1, sh2, sc2, g2 = [mod[l, :, k] for k in range(6)]
        w_qkv, w_u, w_f, w_gt = prep_w_in(w_in, l)
        wq, wkv = _layer_weights(w_uq[l], w_ukv[l])
        nw1 = norm1[l].reshape(1, D)
        if l == 0:
            h = modnorm(xt, nw1, sh1, sc1)
        else:
            xt, h = modnorm_moe(xt, ro, ys, g2_prev, nw1, sh1, sc1)

        qkv = proj(_proj_plain_kernel, h, w_qkv, QKV_COLS, QKV_COLS, "proj_qkv")
        z = proj(_proj_glu_kernel, h, w_u, CONV_CH, 2 * CONV_CH, "proj_glu")
        fc, fs = proj(_proj_four_kernel, h, w_f, FCH, FCH, "proj_four", extra=(cs_ch,),
                      n_outputs=2)
        gs = proj(_proj_sigmoid_kernel, h, w_gt, 3 * D, D, "proj_gate")

        q = q_proj(qkv, q_norm[l].reshape(1, Q_LORA), wq, tabq)
        k, v = kv_proj(qkv, kv_norm[l].reshape(1, KV_LORA), wkv, tabk)
        a = attention(q, k, v)

        w_pad = jnp.zeros((32, CONV_CH), F32).at[:CONV_W].set(conv_dw[l])
        cv = conv_module(z, w_pad, conv_dw_b[l].reshape(1, CONV_CH),
                         conv_ln_g[l].reshape(1, CONV_CH), conv_ln_b[l].reshape(1, CONV_CH))
        fo = fourier(fc, fs, a_lat, a_ctx)

        m = merge1(a, cv, fo, gs, w_mla_out[l].astype(BF16), w_conv_out[l].astype(BF16),
                   w_four_out[l].astype(BF16))
        wr = jnp.zeros((D, ROUTER_PAD), BF16).at[:, :N_EXP].set(w_router[l].astype(BF16))
        xt, h2, logits = merge2(m, w_out[l].astype(BF16), xt, g1, norm2[l].reshape(1, D), sh2, sc2,
                                wr)
        last = l == L - 1
        ro, ys = moe(h2, logits, b_router[l], w_gate_up, b_gate_up, w_down, b_down, l, last)
        g2_prev = g2

    out = final_norm(xt, ro, ys, g2_prev, norm_final.reshape(1, D))
    return out.reshape(B, SEQ, D)
```

```python
import functools

import numpy as np
import jax
import jax.numpy as jnp
from jax import lax
from jax.experimental import pallas as pl
from jax.experimental.pallas import tpu as pltpu

F32 = jnp.float32
BF16 = jnp.bfloat16

D = 2048
B = 4
SEQ = 2048
CTX = 256
NB = CTX + SEQ
T = B * NB
TILE = 256
TPB = NB // TILE
NT = T // TILE
GRID_W = 64
H = 16
Q_LORA = 512
KV_LORA = 512
QK_NOPE = 128
QK_ROPE = 64
V_DIM = 128
CONV_CH = 1024
CONV_W = 31
FG = 4
FGC = 256
FCH = FG * FGC
N_EXP = 32
TOP_K = 4
D_EXP = 1024
ALPHA = 1.702
LIMIT = 7.0
EPS = 1e-6
ROPE_BASE = 10000.0
QK_SCALE = float((QK_NOPE + QK_ROPE) ** -0.5)
EXPERT_BM = 256
ROUTER_PAD = 128

ARB = "arbitrary"


def _cparams(n_axes, vmem_mb):
    return pltpu.CompilerParams(dimension_semantics=(ARB,) * n_axes,
                                vmem_limit_bytes=vmem_mb << 20)


def _mod_row(i):
    return jnp.where(i % TPB == 0, B, i // TPB)


def _ada_kernel(c_ref, w_ref, b_ref, o_ref):
    c = c_ref[...]
    s = (c * jax.nn.sigmoid(c)).astype(BF16)
    o_ref[0] = jnp.dot(s, w_ref[0].astype(BF16), preferred_element_type=F32) + b_ref[0]


def ada_mod(cc, w_ada, b_ada):
    L, _, N = w_ada.shape
    tn = 1024
    return pl.pallas_call(
        _ada_kernel,
        grid=(L, N // tn),
        in_specs=[pl.BlockSpec((8, D), lambda l, j: (0, 0)),
                  pl.BlockSpec((1, D, tn), lambda l, j: (l, 0, j)),
                  pl.BlockSpec((1, 1, tn), lambda l, j: (l, 0, j))],
        out_specs=pl.BlockSpec((1, 8, tn), lambda l, j: (l, 0, j)),
        out_shape=jax.ShapeDtypeStruct((L, 8, N), F32),
        compiler_params=_cparams(2, 40),
        name="ada_mod",
    )(cc, w_ada, b_ada.reshape(L, 1, N))


def _rms(x, w):
    ms = jnp.mean(x * x, axis=-1, keepdims=True)
    return x * lax.rsqrt(ms + EPS) * w


def _modnorm_kernel(x_ref, nw_ref, sh_ref, sc_ref, h_ref):
    y = _rms(x_ref[...], nw_ref[...])
    h_ref[...] = (y * (1.0 + sc_ref[0]) + sh_ref[0]).astype(BF16)


def _vec_spec():
    return pl.BlockSpec((1, 1, D), lambda i: (_mod_row(i), 0, 0))


def _row_spec(cols=D):
    return pl.BlockSpec((TILE, cols), lambda i: (i, 0))


def _const_spec(shape):
    return pl.BlockSpec(shape, lambda i: (0,) * len(shape))


def modnorm(x, nw, sh, sc):
    return pl.pallas_call(
        _modnorm_kernel,
        grid=(NT,),
        in_specs=[_row_spec(), _const_spec((1, D)), _vec_spec(), _vec_spec()],
        out_specs=_row_spec(),
        out_shape=jax.ShapeDtypeStruct((T, D), BF16),
        compiler_params=_cparams(1, 32),
        name="modnorm",
    )(x, nw, sh, sc)


PROJ_TM = 768


def _proj_plain_kernel(a_ref, w_ref, o_ref):
    o_ref[...] = jnp.dot(a_ref[...], w_ref[...], preferred_element_type=F32).astype(o_ref.dtype)


def _proj_sigmoid_kernel(a_ref, w_ref, o_ref):
    acc = jnp.dot(a_ref[...], w_ref[...], preferred_element_type=F32)
    o_ref[...] = jax.nn.sigmoid(acc).astype(o_ref.dtype)


def _proj_glu_kernel(a_ref, w_ref, o_ref):
    acc = jnp.dot(a_ref[...], w_ref[...], preferred_element_type=F32)
    o_ref[...] = (acc[:, :CONV_CH] * jax.nn.sigmoid(acc[:, CONV_CH:])).astype(o_ref.dtype)


def _proj_four_kernel(a_ref, w_ref, cs_ref, fc_ref, fs_ref):
    f = jnp.dot(a_ref[...], w_ref[...], preferred_element_type=F32).astype(BF16)
    cs = cs_ref[...].astype(BF16)
    for g in range(FG):
        r = jnp.dot(f[:, g * FGC:(g + 1) * FGC], cs, preferred_element_type=F32)
        fc_ref[:, g * FGC:(g + 1) * FGC] = r[:, :FGC].astype(BF16)
        fs_ref[:, g * FGC:(g + 1) * FGC] = r[:, FGC:].astype(BF16)


def proj(kernel, h, w, n_out, tn, name, extra=(), n_outputs=1, vmem_mb=48):
    K, N = w.shape
    tm = PROJ_TM
    tn_out = n_out // (N // tn)
    in_specs = [pl.BlockSpec((tm, K), lambda j, i: (i, 0)),
                pl.BlockSpec((K, tn), lambda j, i: (0, j))]
    for e in extra:
        in_specs.append(pl.BlockSpec(e.shape, lambda j, i, nd=e.ndim: (0,) * nd))
    out_spec = pl.BlockSpec((tm, tn_out), lambda j, i: (i, j))
    out_shape = jax.ShapeDtypeStruct((T, n_out), BF16)
    if n_outputs > 1:
        out_spec = [out_spec] * n_outputs
        out_shape = [out_shape] * n_outputs
    return pl.pallas_call(
        kernel,
        grid=(N // tn, T // tm),
        in_specs=in_specs,
        out_specs=out_spec,
        out_shape=out_shape,
        compiler_params=_cparams(2, vmem_mb),
        name=name,
    )(h, w, *extra)


MLA_TM = 768
MLA_HG = 4


def _lane_lt64(shape):
    return lax.broadcasted_iota(jnp.int32, shape, 1) < QK_ROPE


def _qproj_kernel(cq_ref, nw_ref, w_ref, tab_ref, q_ref, cqn_ref):
    @pl.when(pl.program_id(1) == 0)
    def _():
        cqn_ref[...] = _rms(cq_ref[...].astype(F32), nw_ref[...]).astype(BF16)

    for hh in range(MLA_HG):
        y = jnp.dot(cqn_ref[...], w_ref[hh], preferred_element_type=F32)
        a = y[:, QK_NOPE:] * tab_ref[...]
        q_ref[hh, :, :QK_NOPE] = (y[:, :QK_NOPE] * QK_SCALE).astype(BF16)
        q_ref[hh, :, QK_NOPE:] = (a + pltpu.roll(a, QK_ROPE, 1)).astype(BF16)


def q_proj(qkv, nw, wq, tabq):
    tm = MLA_TM
    return pl.pallas_call(
        _qproj_kernel,
        grid=(T // tm, H // MLA_HG),
        in_specs=[pl.BlockSpec((tm, Q_LORA), lambda i, h: (i, 0)),
                  pl.BlockSpec((1, Q_LORA), lambda i, h: (0, 0)),
                  pl.BlockSpec((MLA_HG, Q_LORA, 256), lambda i, h: (h, 0, 0)),
                  pl.BlockSpec((tm, 128), lambda i, h: (i % (NB // tm), 0))],
        out_specs=pl.BlockSpec((MLA_HG, tm, 256), lambda i, h: (h, i, 0)),
        out_shape=jax.ShapeDtypeStruct((H, T, 256), BF16),
        scratch_shapes=[pltpu.VMEM((tm, Q_LORA), BF16)],
        compiler_params=_cparams(2, 32),
        name="q_proj",
    )(qkv, nw, wq, tabq)


def _kvproj_kernel(ckv_ref, kr_ref, nw_ref, w_ref, tab_ref, k_ref, v_ref, ckvn_ref, k2_ref):
    @pl.when(pl.program_id(1) == 0)
    def _():
        ckvn_ref[...] = _rms(ckv_ref[...].astype(F32), nw_ref[...]).astype(BF16)
        a = kr_ref[...].astype(F32) * tab_ref[...]
        s = a + pltpu.roll(a, QK_ROPE, 1)
        k2_ref[...] = jnp.where(_lane_lt64(s.shape), s, 0.0).astype(BF16)

    for hh in range(MLA_HG):
        y = jnp.dot(ckvn_ref[...], w_ref[hh], preferred_element_type=F32)
        k_ref[hh, :, :QK_NOPE] = y[:, :QK_NOPE].astype(BF16)
        k_ref[hh, :, QK_NOPE:] = k2_ref[...]
        v_ref[hh] = y[:, QK_NOPE:].astype(BF16)


def kv_proj(qkv, nw, wkv, tabk):
    tm = MLA_TM
    return pl.pallas_call(
        _kvproj_kernel,
        grid=(T // tm, H // MLA_HG),
        in_specs=[pl.BlockSpec((tm, KV_LORA), lambda i, h: (i, 1)),
                  pl.BlockSpec((tm, 128), lambda i, h: (i, (Q_LORA + KV_LORA) // 128)),
                  pl.BlockSpec((1, KV_LORA), lambda i, h: (0, 0)),
                  pl.BlockSpec((MLA_HG, KV_LORA, 256), lambda i, h: (h, 0, 0)),
                  pl.BlockSpec((tm, 128), lambda i, h: (i % (NB // tm), 0))],
        out_specs=[pl.BlockSpec((MLA_HG, tm, 256), lambda i, h: (h, i, 0)),
                   pl.BlockSpec((MLA_HG, tm, V_DIM), lambda i, h: (h, i, 0))],
        out_shape=[jax.ShapeDtypeStruct((H, T, 256), BF16),
                   jax.ShapeDtypeStruct((H, T, V_DIM), BF16)],
        scratch_shapes=[pltpu.VMEM((tm, KV_LORA), BF16), pltpu.VMEM((tm, 128), BF16)],
        compiler_params=_cparams(2, 32),
        name="kv_proj",
    )(qkv, qkv, nw, wkv, tabk)


ATT_TQ = 256


def _attend(q, k, v):
    s = lax.dot_general(q, k, (((1,), (1,)), ((), ())), preferred_element_type=F32)
    m = jnp.max(s, axis=-1, keepdims=True)
    p = jnp.exp(s - m)
    l = jnp.sum(p, axis=-1, keepdims=True)
    o = jnp.dot(p.astype(BF16), v, preferred_element_type=F32)
    return (o / l).astype(BF16)


def _attn_kernel(q_ref, k_ref, v_ref, o_ref):
    o_ref[:CTX, :] = _attend(q_ref[0, :CTX, :], k_ref[0, :CTX, :], v_ref[0, :CTX, :])
    for c in range(SEQ // ATT_TQ):
        r0 = CTX + c * ATT_TQ
        o_ref[r0:r0 + ATT_TQ, :] = _attend(q_ref[0, r0:r0 + ATT_TQ, :], k_ref[0], v_ref[0])


def attention(q, k, v):
    return pl.pallas_call(
        _attn_kernel,
        grid=(B, H),
        in_specs=[pl.BlockSpec((1, NB, 256), lambda b, h: (h, b, 0)),
                  pl.BlockSpec((1, NB, 256), lambda b, h: (h, b, 0)),
                  pl.BlockSpec((1, NB, V_DIM), lambda b, h: (h, b, 0))],
        out_specs=pl.BlockSpec((NB, V_DIM), lambda b, h: (b, h)),
        out_shape=jax.ShapeDtypeStruct((T, H * V_DIM), BF16),
        compiler_params=_cparams(2, 48),
        name="attention",
    )(q, k, v)


HALO = 16
CONV_RC = 32
SUBLANES = 8
CONV_SH_ROWS = TILE + 2 * HALO - SUBLANES


def _conv_kernel(zp_ref, zc_ref, zn_ref, w_ref, b_ref, g_ref, be_ref, o_ref, buf_ref, sh_ref):
    j = pl.program_id(0) % TPB
    prev_ok = j >= 2
    next_ok = jnp.logical_and(j >= 1, j <= TPB - 2)
    buf_ref[0:HALO, :] = jnp.where(prev_ok, zp_ref[...].astype(F32), 0.0)
    buf_ref[HALO:HALO + TILE, :] = zc_ref[...].astype(F32)
    buf_ref[HALO + TILE:, :] = jnp.where(next_ok, zn_ref[...].astype(F32), 0.0)
    for s in range(1, SUBLANES):
        sh_ref[s - 1] = buf_ref[s:s + CONV_SH_ROWS, :]
    off = HALO - CONV_W // 2
    for rc in range(TILE // CONV_RC):
        r0 = rc * CONV_RC
        acc = jnp.zeros((CONV_RC, CONV_CH), F32) + b_ref[...]
        for t in range(CONV_W):
            q, s = divmod(off + t, SUBLANES)
            a0 = r0 + q * SUBLANES
            if s == 0:
                tap = buf_ref[a0:a0 + CONV_RC, :]
            else:
                tap = sh_ref[s - 1, a0:a0 + CONV_RC, :]
            acc = acc + tap * w_ref[t:t + 1, :]
        mu = jnp.mean(acc, axis=-1, keepdims=True)
        d = acc - mu
        var = jnp.mean(d * d, axis=-1, keepdims=True)
        zn = d * lax.rsqrt(var + EPS) * g_ref[...] + be_ref[...]
        o_ref[r0:r0 + CONV_RC, :] = (zn * jax.nn.sigmoid(zn)).astype(BF16)


def conv_module(z, w_pad, b, g, be):
    nh = T // HALO
    return pl.pallas_call(
        _conv_kernel,
        grid=(NT,),
        in_specs=[pl.BlockSpec((HALO, CONV_CH),
                               lambda i: (jnp.maximum(i * (TILE // HALO) - 1, 0), 0)),
                  pl.BlockSpec((TILE, CONV_CH), lambda i: (i, 0)),
                  pl.BlockSpec((HALO, CONV_CH),
                               lambda i: (jnp.minimum((i + 1) * (TILE // HALO), nh - 1), 0)),
                  _const_spec((32, CONV_CH)), _const_spec((1, CONV_CH)),
                  _const_spec((1, CONV_CH)), _const_spec((1, CONV_CH))],
        out_specs=pl.BlockSpec((TILE, CONV_CH), lambda i: (i, 0)),
        out_shape=jax.ShapeDtypeStruct((T, CONV_CH), BF16),
        scratch_shapes=[pltpu.VMEM((TILE + 2 * HALO, CONV_CH), F32),
                        pltpu.VMEM((SUBLANES - 1, CONV_SH_ROWS, CONV_CH), F32)],
        compiler_params=_cparams(1, 40),
        name="conv_module",
    )(z, z, z, w_pad, b, g, be)


def _dft_tables():
    def cs(n):
        k = np.arange(n, dtype=np.int64)
        ang = 2.0 * np.pi * ((k[:, None] * k[None, :]) % n).astype(np.float64) / n
        return np.cos(ang), np.sin(ang)

    cl, sl = cs(SEQ)
    cc, sc = cs(CTX)
    cg, sg = cs(FGC)
    a_lat = np.concatenate([cl, -sl], axis=1).astype(np.float32)
    a_ctx = np.concatenate([cc, -sc], axis=1).astype(np.float32)
    cs_ch = np.concatenate([cg, sg], axis=1).astype(np.float32)
    return jnp.asarray(a_lat), jnp.asarray(a_ctx), jnp.asarray(cs_ch)


LAT_SCALE = float((SEQ * FGC) ** -0.5)
CTX_SCALE = float((CTX * FGC) ** -0.5)


def _fourier_kernel(al_ref, ac_ref, fc_ref, fs_ref, o_ref):
    m = pl.program_id(1)

    @pl.when(m == 0)
    def _():
        r = jnp.dot(ac_ref[:, :CTX].astype(BF16), fc_ref[0, :CTX, :], preferred_element_type=F32)
        r = r + jnp.dot(ac_ref[:, CTX:].astype(BF16), fs_ref[0, :CTX, :],
                        preferred_element_type=F32)
        o_ref[...] = (r * CTX_SCALE).astype(BF16)

    @pl.when(m > 0)
    def _():
        r = jnp.dot(al_ref[:, :SEQ].astype(BF16), fc_ref[0, CTX:, :], preferred_element_type=F32)
        r = r + jnp.dot(al_ref[:, SEQ:].astype(BF16), fs_ref[0, CTX:, :],
                        preferred_element_type=F32)
        o_ref[...] = (r * LAT_SCALE).astype(BF16)


def fourier(fc, fs, a_lat, a_ctx):
    return pl.pallas_call(
        _fourier_kernel,
        grid=(B, TPB),
        in_specs=[pl.BlockSpec((TILE, 2 * SEQ), lambda b, m: (jnp.maximum(m - 1, 0), 0)),
                  pl.BlockSpec((CTX, 2 * CTX), lambda b, m: (0, 0)),
                  pl.BlockSpec((1, NB, FCH), lambda b, m: (b, 0, 0)),
                  pl.BlockSpec((1, NB, FCH), lambda b, m: (b, 0, 0))],
        out_specs=pl.BlockSpec((TILE, FCH), lambda b, m: (b * TPB + m, 0)),
        out_shape=jax.ShapeDtypeStruct((T, FCH), BF16),
        compiler_params=_cparams(2, 48),
        name="fourier",
    )(a_lat, a_ctx, fc.reshape(B, NB, FCH), fs.reshape(B, NB, FCH))


MERGE_TM = 512
MERGE_TN = 1024


def _merge1_kernel(a_ref, cv_ref, fo_ref, g0_ref, g1_ref, g2_ref, wm_ref, wc_ref, wf_ref, o_ref):
    m = g0_ref[...].astype(F32) * jnp.dot(a_ref[...], wm_ref[...], preferred_element_type=F32)
    m = m + g1_ref[...].astype(F32) * jnp.dot(cv_ref[...], wc_ref[...],
                                               preferred_element_type=F32)
    m = m + g2_ref[...].astype(F32) * jnp.dot(fo_ref[...], wf_ref[...],
                                               preferred_element_type=F32)
    o_ref[...] = m.astype(BF16)


def merge1(a, cv, fo, gs, wm, wc, wf):
    tm, tn = MERGE_TM, MERGE_TN
    nj = D // tn

    def gspec(k):
        return pl.BlockSpec((tm, tn), lambda j, i: (i, k * nj + j))

    return pl.pallas_call(
        _merge1_kernel,
        grid=(nj, T // tm),
        in_specs=[pl.BlockSpec((tm, H * V_DIM), lambda j, i: (i, 0)),
                  pl.BlockSpec((tm, CONV_CH), lambda j, i: (i, 0)),
                  pl.BlockSpec((tm, FCH), lambda j, i: (i, 0)),
                  gspec(0), gspec(1), gspec(2),
                  pl.BlockSpec((H * V_DIM, tn), lambda j, i: (0, j)),
                  pl.BlockSpec((CONV_CH, tn), lambda j, i: (0, j)),
                  pl.BlockSpec((FCH, tn), lambda j, i: (0, j))],
        out_specs=pl.BlockSpec((tm, tn), lambda j, i: (i, j)),
        out_shape=jax.ShapeDtypeStruct((T, D), BF16),
        compiler_params=_cparams(2, 48),
        name="merge1",
    )(a, cv, fo, gs, gs, gs, wm, wc, wf)


def _merge2_kernel(m_ref, w_ref, x_ref, g_ref, nw_ref, sh_ref, sc_ref, wr_ref,
                   xo_ref, h_ref, lg_ref):
    acc = jnp.dot(m_ref[...], w_ref[...], preferred_element_type=F32)
    x = x_ref[...] + g_ref[0] * acc
    xo_ref[...] = x
    h = _rms(x, nw_ref[...]) * (1.0 + sc_ref[0]) + sh_ref[0]
    h_ref[...] = h
    lg_ref[...] = jnp.dot(h.astype(BF16), wr_ref[...], preferred_element_type=F32)


def merge2(m, w_out, x, g, nw, sh, sc, wr):
    return pl.pallas_call(
        _merge2_kernel,
        grid=(NT,),
        in_specs=[_row_spec(), _const_spec((D, D)), _row_spec(), _vec_spec(),
                  _const_spec((1, D)), _vec_spec(), _vec_spec(), _const_spec((D, ROUTER_PAD))],
        out_specs=[_row_spec(), _row_spec(), _row_spec(ROUTER_PAD)],
        out_shape=[jax.ShapeDtypeStruct((T, D), F32), jax.ShapeDtypeStruct((T, D), F32),
                   jax.ShapeDtypeStruct((T, ROUTER_PAD), F32)],
        compiler_params=_cparams(1, 52),
        name="merge2",
    )(m, w_out, x, g, nw, sh, sc, wr)


W_PARTS = 8
GU_ROWS = D // W_PARTS
DN_ROWS = D_EXP // W_PARTS
N_STAGE = 2
GATHER_UNROLL = 8
X_SLOTS = 3
ISSUE_GROUPS = 4


def _expert_kernel(be_ref, slot_ref, nxt_ref, first_ref, c0_ref, nch_ref, nv_ref, tok_ref,
                   bgu_ref, bd_ref, h_hbm, wgu_hbm, wd_hbm, o_ref,
                   wgu_buf, wd_buf, st_gu, st_d, x_buf, sem_gu, sem_d, sem_x, *, layer):
    i = pl.program_id(0)

    def row_copy(t, r, xs):
        return pltpu.make_async_copy(h_hbm.at[pl.ds(t, 1), :], x_buf.at[xs, pl.ds(r, 1), :],
                                     sem_x.at[xs])

    def start_rows(blk, xs):
        base = blk * EXPERT_BM

        def body(r, carry):
            row_copy(tok_ref[base + r], r, xs).start()
            return carry

        lax.fori_loop(0, EXPERT_BM, body, 0, unroll=GATHER_UNROLL)

    def wait_rows(xs):
        def body(r, carry):
            row_copy(0, r, xs).wait()
            return carry

        lax.fori_loop(0, EXPERT_BM, body, 0, unroll=GATHER_UNROLL)

    def part_copies(e, c, b):
        gu = pltpu.make_async_copy(wgu_hbm.at[layer, e, pl.ds(c * GU_ROWS, GU_ROWS), :],
                                   st_gu.at[b], sem_gu.at[b])
        dn = pltpu.make_async_copy(wd_hbm.at[layer, e, pl.ds(c * DN_ROWS, DN_ROWS), :],
                                   st_d.at[b], sem_d.at[b])
        return gu, dn

    def start_part(e, c, b):
        gu, dn = part_copies(e, c, b)
        gu.start(priority=1)
        dn.start(priority=1)

    def take_part(e, c, dst):
        b = lax.rem(c, N_STAGE)
        gu, dn = part_copies(e, c, b)
        gu.wait()
        dn.wait()
        wgu_buf[dst, pl.ds(pl.multiple_of(c * GU_ROWS, GU_ROWS), GU_ROWS), :] = (
            st_gu[b].astype(BF16))
        wd_buf[dst, pl.ds(pl.multiple_of(c * DN_ROWS, DN_ROWS), DN_ROWS), :] = (
            st_d[b].astype(BF16))

        @pl.when(c + N_STAGE < W_PARTS)
        def _():
            start_part(e, c + N_STAGE, b)

    @pl.when(i < nv_ref[0])
    def _():
        e = be_ref[i]
        s = slot_ref[i]
        nx = nxt_ref[i]

        xs = lax.rem(i, X_SLOTS)

        @pl.when(i == 0)
        def _():
            for a in range(X_SLOTS - 1):
                @pl.when(a < nv_ref[0])
                def _():
                    start_rows(a, a)
            for b in range(N_STAGE):
                start_part(e, b, b)
            for c in range(W_PARTS):
                take_part(e, jnp.int32(c), s)

        @pl.when(jnp.logical_and(first_ref[i] == 1, nx >= 0))
        def _():
            for b in range(N_STAGE):
                start_part(nx, b, b)

        wait_rows(xs)

        ahead = i + (X_SLOTS - 1)
        axs = lax.rem(ahead, X_SLOTS)

        def mlp(fetch_ahead):
            x = x_buf[xs].astype(BF16)
            slab = 2 * D_EXP // ISSUE_GROUPS
            per = EXPERT_BM // ISSUE_GROUPS
            parts = []
            for q in range(ISSUE_GROUPS):
                if fetch_ahead:
                    for r in range(q * per, (q + 1) * per):
                        row_copy(tok_ref[ahead * EXPERT_BM + r], r, axs).start()
                parts.append(jnp.dot(x, wgu_buf[s, :, q * slab:(q + 1) * slab],
                                     preferred_element_type=F32)
                             + bgu_ref[0, :, q * slab:(q + 1) * slab])
            hgu = jnp.concatenate(parts, axis=1)
            glu = jnp.minimum(hgu[:, :D_EXP], LIMIT)
            lin = jnp.clip(hgu[:, D_EXP:], -LIMIT, LIMIT)
            act = glu * jax.nn.sigmoid(ALPHA * glu) * (lin + 1.0)
            y = jnp.dot(act.astype(BF16), wd_buf[s], preferred_element_type=F32) + bd_ref[0]
            o_ref[...] = y.astype(BF16)

        @pl.when(ahead < nv_ref[0])
        def _():
            mlp(True)

        @pl.when(ahead >= nv_ref[0])
        def _():
            mlp(False)

        def body(k, carry):
            take_part(nx, c0_ref[i] + k, 1 - s)
            return carry

        lax.fori_loop(0, nch_ref[i], body, 0)

    @pl.when(i >= nv_ref[0])
    def _():
        o_ref[...] = jnp.zeros(o_ref.shape, BF16)


def _expert_plan(block_e, n_valid, nblk):
    idx = jnp.arange(nblk, dtype=jnp.int32)
    valid = idx < n_valid
    prev_e = jnp.concatenate([block_e[:1], block_e[:-1]])
    first = jnp.logical_and(valid, jnp.logical_or(idx == 0, block_e != prev_e))
    run_id = jnp.cumsum(first.astype(jnp.int32)) - 1
    n_runs = jnp.sum(first.astype(jnp.int32))
    member = jnp.logical_and(run_id[None, :] == idx[:, None], valid[None, :])
    run_len = jnp.sum(member.astype(jnp.int32), axis=1)
    run_first = jnp.min(jnp.where(member, idx[None, :], nblk), axis=1)
    run_e = block_e[jnp.minimum(run_first, nblk - 1)]
    rlen = jnp.maximum(run_len[run_id], 1)
    j = idx - run_first[run_id]
    has_next = jnp.logical_and(valid, run_id + 1 < n_runs)
    nxt = jnp.where(has_next, run_e[jnp.minimum(run_id + 1, nblk - 1)], -1)
    c_lo = (W_PARTS * j) // rlen
    c_hi = (W_PARTS * (j + 1)) // rlen
    nch = jnp.where(has_next, c_hi - c_lo, 0)
    slot = run_id % 2
    cast = lambda a: a.astype(jnp.int32)
    return cast(slot), cast(nxt), cast(first), cast(c_lo), cast(nch)


def experts(block_e, n_valid, tok_buf, hf, wgu, bgu, wd, bd, layer):
    n_rows = tok_buf.shape[0]
    nblk = n_rows // EXPERT_BM
    slot, nxt, first, c0, nch = _expert_plan(block_e, n_valid[0], nblk)

    def bmap(i, *s):
        return (layer * N_EXP + s[0][i], 0, 0)

    any_spec = pl.BlockSpec(memory_space=pl.ANY)
    grid_spec = pltpu.PrefetchScalarGridSpec(
        num_scalar_prefetch=8,
        grid=(nblk,),
        in_specs=[pl.BlockSpec((1, 1, 2 * D_EXP), bmap),
                  pl.BlockSpec((1, 1, D), bmap),
                  any_spec, any_spec, any_spec],
        out_specs=pl.BlockSpec((EXPERT_BM, D), lambda i, *s: (i, 0)),
        scratch_shapes=[pltpu.VMEM((2, D, 2 * D_EXP), BF16),
                        pltpu.VMEM((2, D_EXP, D), BF16),
                        pltpu.VMEM((N_STAGE, GU_ROWS, 2 * D_EXP), F32),
                        pltpu.VMEM((N_STAGE, DN_ROWS, D), F32),
                        pltpu.VMEM((X_SLOTS, EXPERT_BM, D), F32),
                        pltpu.SemaphoreType.DMA((N_STAGE,)),
                        pltpu.SemaphoreType.DMA((N_STAGE,)),
                        pltpu.SemaphoreType.DMA((X_SLOTS,))],
    )
    L = wgu.shape[0]
    return pl.pallas_call(
        functools.partial(_expert_kernel, layer=layer),
        grid_spec=grid_spec,
        out_shape=jax.ShapeDtypeStruct((n_rows, D), BF16),
        compiler_params=_cparams(1, 56),
        name="experts",
    )(block_e, slot, nxt, first, c0, nch, n_valid, tok_buf,
      bgu.reshape(L * N_EXP, 1, 2 * D_EXP), bd.reshape(L * N_EXP, 1, D), hf, wgu, wd)


NEG_BIG = -1e30
RO_E, RO_RANK, RO_GATE = 0, TOP_K, 2 * TOP_K


def _router_kernel(lg_ref, b_ref, tri_ref, ro_ref, cnt_ref, base_ref):
    @pl.when(pl.program_id(0) == 0)
    def _():
        base_ref[...] = jnp.zeros(base_ref.shape, F32)

    lane = lax.broadcasted_iota(jnp.int32, (TILE, ROUTER_PAD), 1)
    lanef = lane.astype(F32)
    lg = jnp.where(lane < N_EXP, lg_ref[...] + b_ref[...], NEG_BIG)
    vals, idxs, hots = [], [], []
    for _ in range(TOP_K):
        m = jnp.max(lg, axis=-1, keepdims=True)
        idx = jnp.min(jnp.where(lg == m, lanef, float(ROUTER_PAD)), axis=-1, keepdims=True)
        hot = lanef == idx
        lg = jnp.where(hot, NEG_BIG, lg)
        vals.append(m)
        idxs.append(idx)
        hots.append(hot)
    ex = [jnp.exp(v - vals[0]) for v in vals]
    den = ex[0] + ex[1] + ex[2] + ex[3]
    chosen = jnp.zeros((TILE, ROUTER_PAD), F32)
    for hot in hots:
        chosen = jnp.where(hot, 1.0, chosen)
    before = jnp.dot(tri_ref[...], chosen.astype(BF16), preferred_element_type=F32)
    tot = before + base_ref[0:1, :]
    rec = jnp.zeros((TILE, ROUTER_PAD), F32)
    for k in range(TOP_K):
        rank = jnp.sum(jnp.where(hots[k], tot, 0.0), axis=-1, keepdims=True)
        rec = jnp.where(lane == RO_E + k, idxs[k], rec)
        rec = jnp.where(lane == RO_RANK + k, rank, rec)
        rec = jnp.where(lane == RO_GATE + k, ex[k] / den, rec)
    ro_ref[...] = rec
    base_ref[...] = base_ref[...] + jnp.sum(chosen, axis=0, keepdims=True)
    cnt_ref[...] = base_ref[...]


def router(logits, b_pad, tri, latent_only):
    if latent_only:
        ntl = SEQ // TILE
        n_tiles = B * ntl
        in_map = lambda i: ((i // ntl) * TPB + 1 + i % ntl, 0)
    else:
        n_tiles = NT
        in_map = lambda i: (i, 0)
    return pl.pallas_call(
        _router_kernel,
        grid=(n_tiles,),
        in_specs=[pl.BlockSpec((TILE, ROUTER_PAD), in_map),
                  _const_spec((1, ROUTER_PAD)), _const_spec((TILE, TILE))],
        out_specs=[_row_spec(ROUTER_PAD), _const_spec((8, ROUTER_PAD))],
        out_shape=[jax.ShapeDtypeStruct((n_tiles * TILE, ROUTER_PAD), F32),
                   jax.ShapeDtypeStruct((8, ROUTER_PAD), F32)],
        scratch_shapes=[pltpu.VMEM((8, ROUTER_PAD), F32)],
        compiler_params=_cparams(1, 32),
        name="router",
    )(logits, b_pad, tri)


def moe(hf, logits, b_router, wgu, bgu, wd, bd, layer, latent_only):
    b_pad = jnp.zeros((1, ROUTER_PAD), F32).at[0, :N_EXP].set(b_router)
    tri = jnp.asarray(np.tril(np.ones((TILE, TILE), np.float32), -1), dtype=BF16)
    ro, cnt = router(logits, b_pad, tri, latent_only)
    n_tok = ro.shape[0]
    if latent_only:
        tok_ids = (jnp.arange(B, dtype=jnp.int32)[:, None] * NB + CTX
                   + jnp.arange(SEQ, dtype=jnp.int32)[None, :]).reshape(-1)
    else:
        tok_ids = jnp.arange(T, dtype=jnp.int32)
    e = ro[:, RO_E:RO_E + TOP_K].astype(jnp.int32)
    rank = ro[:, RO_RANK:RO_RANK + TOP_K].astype(jnp.int32)
    counts = cnt[0, :N_EXP].astype(jnp.int32)
    padded = (counts + EXPERT_BM - 1) // EXPERT_BM * EXPERT_BM
    pends = jnp.cumsum(padded)
    pstarts = pends - padded
    dest = pstarts[e] + rank
    n_pair = n_tok * TOP_K
    nblk = n_pair // EXPERT_BM + N_EXP
    n_rows = nblk * EXPERT_BM
    tok_buf = jnp.zeros((n_rows,), jnp.int32).at[dest.reshape(-1)].set(jnp.repeat(tok_ids, TOP_K))
    n_valid = (pends[-1] // EXPERT_BM).astype(jnp.int32)
    blk = jnp.arange(nblk, dtype=jnp.int32)
    block_e = jnp.minimum(jnp.sum((pends[None, :] <= blk[:, None] * EXPERT_BM).astype(jnp.int32),
                                  axis=1), N_EXP - 1)
    block_e = jnp.where(blk < n_valid, block_e, block_e[n_valid - 1])
    yb = experts(block_e.astype(jnp.int32), n_valid.reshape(1), tok_buf, hf, wgu, bgu, wd, bd,
                 layer)
    ys = yb[dest.T.reshape(-1)]
    return ro, ys


def _combine(ro_ref, y_refs):
    ro = ro_ref[...]
    acc = ro[:, RO_GATE:RO_GATE + 1] * y_refs[0][...].astype(F32)
    for k in range(1, TOP_K):
        acc = acc + ro[:, RO_GATE + k:RO_GATE + k + 1] * y_refs[k][...].astype(F32)
    return acc


def _modnorm_moe_kernel(x_ref, ro_ref, y0_ref, y1_ref, y2_ref, y3_ref, g_ref, nw_ref, sh_ref,
                        sc_ref, xo_ref, h_ref):
    x = x_ref[...] + g_ref[0] * _combine(ro_ref, (y0_ref, y1_ref, y2_ref, y3_ref))
    xo_ref[...] = x
    y = _rms(x, nw_ref[...])
    h_ref[...] = (y * (1.0 + sc_ref[0]) + sh_ref[0]).astype(BF16)


def modnorm_moe(x, ro, ys, g, nw, sh, sc):
    return pl.pallas_call(
        _modnorm_moe_kernel,
        grid=(NT,),
        in_specs=[_row_spec(), _row_spec(ROUTER_PAD)]
                 + [pl.BlockSpec((TILE, D), lambda i, k=k: (k * NT + i, 0)) for k in range(TOP_K)]
                 + [_vec_spec(), _const_spec((1, D)), _vec_spec(), _vec_spec()],
        out_specs=[_row_spec(), _row_spec()],
        out_shape=[jax.ShapeDtypeStruct((T, D), F32), jax.ShapeDtypeStruct((T, D), BF16)],
        compiler_params=_cparams(1, 40),
        name="modnorm_moe",
    )(x, ro, *([ys] * TOP_K), g, nw, sh, sc)


def _final_kernel(x_ref, ro_ref, y0_ref, y1_ref, y2_ref, y3_ref, g_ref, nw_ref, o_ref):
    x = x_ref[...] + g_ref[0] * _combine(ro_ref, (y0_ref, y1_ref, y2_ref, y3_ref))
    o_ref[...] = _rms(x, nw_ref[...])


def final_norm(x, ro, ys, g, nw):
    ntl = SEQ // TILE
    lat = lambda cols: pl.BlockSpec((TILE, cols), lambda b, m: (b * ntl + m, 0))
    return pl.pallas_call(
        _final_kernel,
        grid=(B, ntl),
        in_specs=[pl.BlockSpec((TILE, D), lambda b, m: (b * TPB + 1 + m, 0)),
                  lat(ROUTER_PAD)]
                 + [pl.BlockSpec((TILE, D), lambda b, m, k=k: (k * B * ntl + b * ntl + m, 0))
                    for k in range(TOP_K)]
                 + [pl.BlockSpec((1, 1, D), lambda b, m: (b, 0, 0)),
                    pl.BlockSpec((1, D), lambda b, m: (0, 0))],
        out_specs=lat(D),
        out_shape=jax.ShapeDtypeStruct((B * SEQ, D), F32),
        compiler_params=_cparams(2, 32),
        name="final_norm",
    )(x, ro, *([ys] * TOP_K), g, nw)


def _rope_tables():
    t = np.arange(SEQ)
    row = (t // GRID_W).astype(np.float64)
    col = (t % GRID_W).astype(np.float64)
    half = QK_ROPE // 2
    inv = ROPE_BASE ** (-np.arange(0, half, 2, dtype=np.float64) / half)
    ang = np.concatenate([row[:, None] * inv, col[:, None] * inv], axis=-1)
    cos, sin = np.cos(ang), np.sin(ang)
    lat = np.concatenate([cos, cos, -sin, sin], axis=-1)
    ctx = np.concatenate([np.ones((CTX, QK_ROPE)), np.zeros((CTX, QK_ROPE))], axis=-1)
    tab = np.concatenate([ctx, lat], axis=0).astype(np.float32)
    return jnp.asarray(tab), jnp.asarray(tab * np.float32(QK_SCALE))


def _swap_halves(w):
    half = QK_ROPE // 2
    return jnp.concatenate([w[..., half:], w[..., :half]], axis=-1)


IN_O1 = Q_LORA + KV_LORA
IN_O2 = IN_O1 + QK_ROPE
IN_O3 = IN_O2 + 2 * CONV_CH
IN_O4 = IN_O3 + FCH
IN_COLS = IN_O4 + 3 * D
QKV_COLS = IN_O1 + 2 * QK_ROPE
PREP_TR = 256


def _prep_kernel(w_ref, qkv_ref, u_ref, f_ref, gt_ref):
    qkv_ref[:, :IN_O1] = w_ref[0, :, :IN_O1].astype(BF16)
    kb = w_ref[0, :, IN_O1:IN_O1 + 128]
    lane = lax.broadcasted_iota(jnp.int32, kb.shape, 1)
    half = QK_ROPE // 2
    swapped = jnp.where(lane < QK_ROPE, kb,
                        jnp.where(lane < QK_ROPE + half, pltpu.roll(kb, half, 1),
                                  pltpu.roll(kb, QK_ROPE + half, 1)))
    qkv_ref[:, IN_O1:] = swapped.astype(BF16)
    u_ref[...] = w_ref[0, :, IN_O2:IN_O3].astype(BF16)
    f_ref[...] = w_ref[0, :, IN_O3:IN_O4].astype(BF16)
    gt_ref[...] = w_ref[0, :, IN_O4:].astype(BF16)


def prep_w_in(w_in, l):
    widths = (QKV_COLS, 2 * CONV_CH, FCH, 3 * D)
    return pl.pallas_call(
        _prep_kernel,
        grid=(D // PREP_TR,),
        in_specs=[pl.BlockSpec((1, PREP_TR, IN_COLS), lambda i: (l, i, 0))],
        out_specs=[pl.BlockSpec((PREP_TR, n), lambda i: (i, 0)) for n in widths],
        out_shape=[jax.ShapeDtypeStruct((D, n), BF16) for n in widths],
        compiler_params=_cparams(1, 48),
        name="prep_w_in",
    )(w_in)


def _layer_weights(w_uq, w_ukv):
    wq = w_uq.reshape(Q_LORA, H, QK_NOPE + QK_ROPE)
    wq_r = wq[:, :, QK_NOPE:]
    wq = jnp.concatenate([wq[:, :, :QK_NOPE], wq_r, _swap_halves(wq_r)], axis=-1)
    wq = jnp.transpose(wq, (1, 0, 2)).astype(BF16)
    wkv = jnp.transpose(w_ukv.reshape(KV_LORA, H, QK_NOPE + V_DIM), (1, 0, 2)).astype(BF16)
    return wq, wkv


def kernel(x, c, ctx, c_ctx, w_ada, b_ada, norm1, w_in, q_norm, kv_norm, w_uq, w_ukv, w_mla_out,
           conv_dw, conv_dw_b, conv_ln_g, conv_ln_b, w_conv_out, w_four_out, w_out, norm2,
           w_router, b_router, w_gate_up, b_gate_up, w_down, b_down, norm_final):
    L = w_ada.shape[0]
    xt = jnp.concatenate([ctx, x], axis=1).reshape(T, D)
    cc = jnp.zeros((8, D), F32).at[:B].set(c).at[B].set(c_ctx)
    mod = ada_mod(cc, w_ada, b_ada).reshape(L, 8, 6, 1, D)
    tabk, tabq = _rope_tables()
    a_lat, a_ctx, cs_ch = _dft_tables()

    ro = ys = None
    g2_prev = None
    for l in range(L):
        sh1, sc1, g1, sh2, sc2, g2 = [mod[l, :, k] for k in range(6)]
        w_qkv, w_u, w_f, w_gt = prep_w_in(w_in, l)
        wq, wkv = _layer_weights(w_uq[l], w_ukv[l])
        nw1 = norm1[l].reshape(1, D)
        if l == 0:
            h = modnorm(xt, nw1, sh1, sc1)
        else:
            xt, h = modnorm_moe(xt, ro, ys, g2_prev, nw1, sh1, sc1)

        qkv = proj(_proj_plain_kernel, h, w_qkv, QKV_COLS, QKV_COLS, "proj_qkv")
        z = proj(_proj_glu_kernel, h, w_u, CONV_CH, 2 * CONV_CH, "proj_glu")
        fc, fs = proj(_proj_four_kernel, h, w_f, FCH, FCH, "proj_four", extra=(cs_ch,),
                      n_outputs=2)
        gs = proj(_proj_sigmoid_kernel, h, w_gt, 3 * D, D, "proj_gate")

        q = q_proj(qkv, q_norm[l].reshape(1, Q_LORA), wq, tabq)
        k, v = kv_proj(qkv, kv_norm[l].reshape(1, KV_LORA), wkv, tabk)
        a = attention(q, k, v)

        w_pad = jnp.zeros((32, CONV_CH), F32).at[:CONV_W].set(conv_dw[l])
        cv = conv_module(z, w_pad, conv_dw_b[l].reshape(1, CONV_CH),
                         conv_ln_g[l].reshape(1, CONV_CH), conv_ln_b[l].reshape(1, CONV_CH))
        fo = fourier(fc, fs, a_lat, a_ctx)

        m = merge1(a, cv, fo, gs, w_mla_out[l].astype(BF16), w_conv_out[l].astype(BF16),
                   w_four_out[l].astype(BF16))
        wr = jnp.zeros((D, ROUTER_PAD), BF16).at[:, :N_EXP].set(w_router[l].astype(BF16))
        xt, h2, logits = merge2(m, w_out[l].astype(BF16), xt, g1, norm2[l].reshape(1, D), sh2, sc2,
                                wr)
        last = l == L - 1
        ro, ys = moe(h2, logits, b_router[l], w_gate_up, b_gate_up, w_down, b_down, l, last)
        g2_prev = g2

    out = final_norm(xt, ro, ys, g2_prev, norm_final.reshape(1, D))
    return out.reshape(B, SEQ, D)
```
